```python
import math, functools
import jax, jax.numpy as jnp
from jax import lax
import numpy as np

D_MODEL = 1024
BATCH = 2
SEQ = 8192
DEPTH = 4
DEC_BATCH = 128
DEC_SEQ = 1
PAST_LEN = 2048
PAGE_SIZE = 128

N_HEADS = 8
HEAD_DIM = 64
ATT_WIDTH = N_HEADS * HEAD_DIM
LRU_WIDTH = D_MODEL // 2
LRU_BLOCKS = 8
LRU_BLOCK = LRU_WIDTH // LRU_BLOCKS
MIX_WIDTH = ATT_WIDTH + LRU_WIDTH
CONV_WIDTH = 4
LRU_C = 8.0
D_FF = ((8 * D_MODEL + 3 * 256 - 1) // (3 * 256)) * 256
DILATED = ((128, 1), (512, 4), (2048, 16))
MAX_WINDOW = 2048
BLK = 128
EPS = 1e-6

kernel_name = "hymba_dilated_attn_rglru_decoder_step"


def rms_norm(x, g):
    xf = x.astype(jnp.float32)
    y = xf * lax.rsqrt(jnp.mean(xf * xf, axis=-1, keepdims=True) + EPS) * g.astype(jnp.float32)
    return y.astype(x.dtype)


def alibi_slopes():
    return 2.0 ** (-(8.0 / N_HEADS) * jnp.arange(1, N_HEADS + 1, dtype=jnp.float32))


def branch_prompt(q, k, v, window, dil):
    B, S, H, Dh = q.shape
    span = dil * BLK
    s_pad = -(-S // span) * span
    n_u = s_pad // dil
    nb = n_u // BLK

    def blocks(t):
        t = jnp.pad(t, ((0, 0), (0, s_pad - S), (0, 0), (0, 0)))
        t = t.reshape(B, n_u, dil, H, Dh).transpose(0, 2, 1, 3, 4)
        return t.reshape(B, dil, nb, BLK, H, Dh)

    def with_prev(t):
        prev = jnp.pad(t, ((0, 0), (0, 0), (1, 0), (0, 0), (0, 0), (0, 0)))[:, :, :-1]
        return jnp.concatenate([prev, t], axis=3)

    qb = blocks(q)
    kk = with_prev(blocks(k))
    vv = with_prev(blocks(v))
    steps = jnp.arange(BLK)[:, None] + BLK - jnp.arange(2 * BLK)[None, :]
    in_band = (steps >= 0) & (steps <= window // dil)
    has_key = (jnp.arange(nb)[:, None, None] > 0) | (jnp.arange(2 * BLK)[None, None, :] >= BLK)
    mask = in_band[None] & has_key
    bias = -alibi_slopes()[:, None, None] * (dil * steps).astype(jnp.float32)[None]
    s = jnp.einsum('brnqhe,brnkhe->brnhqk', qb, kk) + bias
    s = jnp.where(mask[:, None], s, -jnp.inf)
    m = jnp.max(s, axis=-1)
    p = jnp.exp(s - m[..., None])
    l = jnp.sum(p, axis=-1)
    acc = jnp.einsum('brnhqk,brnkhe->brnqhe', p, vv)

    def unblock(t):
        t = t.reshape((B, dil, n_u) + t.shape[4:])
        t = jnp.swapaxes(t, 1, 2).reshape((B, s_pad) + t.shape[3:])
        return t[:, :S]

    return unblock(jnp.swapaxes(m, 3, 4)), unblock(jnp.swapaxes(l, 3, 4)), unblock(acc)


def branch_sample(q, k_all, v_all, window, dil, w_buf):
    T = q.shape[1]
    steps = jnp.arange(window // dil + 1)
    idx = w_buf + jnp.arange(T)[:, None] - dil * steps[None, :]
    valid = idx >= 0
    idx = jnp.maximum(idx, 0)
    kg = k_all[:, idx]
    vg = v_all[:, idx]
    bias = -alibi_slopes()[:, None, None] * (dil * steps).astype(jnp.float32)[None, None, :]
    s = jnp.einsum('bthe,btkhe->bhtk', q, kg) + bias
    s = jnp.where(valid[None, None], s, -jnp.inf)
    m = jnp.max(s, axis=-1)
    p = jnp.exp(s - m[..., None])
    l = jnp.sum(p, axis=-1)
    acc = jnp.einsum('bhtk,btkhe->bthe', p, vg)
    return jnp.swapaxes(m, 1, 2), jnp.swapaxes(l, 1, 2), acc


def combine_branches(parts):
    m = jnp.stack([pt[0] for pt in parts])
    l = jnp.stack([pt[1] for pt in parts])
    acc = jnp.stack([pt[2] for pt in parts])
    w = jnp.exp(m - jnp.max(m, axis=0))
    return jnp.sum(w[..., None] * acc, axis=0) / jnp.sum(w * l, axis=0)[..., None]


def attend_prompt(q, k, v):
    return combine_branches([branch_prompt(q, k, v, win, dil) for win, dil in DILATED])


def attend_sample(q, k, v, cache_k, cache_v):
    w_buf = cache_k.shape[1]
    k_all = jnp.concatenate([cache_k.astype(jnp.float32), k], axis=1)
    v_all = jnp.concatenate([cache_v.astype(jnp.float32), v], axis=1)
    return combine_branches([branch_sample(q, k_all, v_all, win, dil, w_buf) for win, dil in DILATED])


def causal_conv(u, prev, w, b):
    T = u.shape[1]
    padded = jnp.concatenate([prev, u], axis=1)
    out = sum(padded[:, j:j + T] * w[j] for j in range(CONV_WIDTH)) + b
    return out, padded[:, -(CONV_WIDTH - 1):]


def block_diag(x, w, b):
    B, T, _ = x.shape
    y = jnp.einsum('btnc,ncd->btnd', x.reshape(B, T, LRU_BLOCKS, LRU_BLOCK), w)
    return y.reshape(B, T, LRU_WIDTH) + b


def rg_lru(x, h0, w_rg, b_rg, w_ig, b_ig, lam):
    r = jax.nn.sigmoid(block_diag(x, w_rg, b_rg))
    i = jax.nn.sigmoid(block_diag(x, w_ig, b_ig))
    log_a = -LRU_C * r * jax.nn.softplus(-lam.astype(jnp.float32))
    a = jnp.exp(log_a)
    b = jnp.sqrt(-jnp.expm1(2.0 * log_a)) * (i * x)

    def comb(e1, e2):
        a1, b1 = e1
        a2, b2 = e2
        return a1 * a2, a2 * b1 + b2

    a_cum, b_cum = lax.associative_scan(comb, (a, b), axis=1)
    h = a_cum * h0[:, None] + b_cum
    return h, h[:, -1]


def swiglu(h, wg, wu, wd):
    return (jax.nn.silu(h @ wg) * (h @ wu)) @ wd


def decoder_layer(x, conv_prev, h0, attend, g_pre_mix, g_post_mix, w_in, w_out, conv_w, conv_b,
                  w_rgate, b_rgate, w_igate, b_igate, lru_lambda, g_att_out, g_lru_out,
                  g_pre_ffn, g_post_ffn, w_ffn_gate, w_ffn_up, w_ffn_down):
    B, T, _ = x.shape
    f32 = jnp.float32
    proj = rms_norm(x, g_pre_mix) @ w_in
    q, k, v, xr, yg = jnp.split(proj, [ATT_WIDTH, 2 * ATT_WIDTH, 3 * ATT_WIDTH, 3 * ATT_WIDTH + LRU_WIDTH], axis=-1)
    k = k.reshape(B, T, N_HEADS, HEAD_DIM)
    v = v.reshape(B, T, N_HEADS, HEAD_DIM)
    qf = q.reshape(B, T, N_HEADS, HEAD_DIM).astype(f32) * (HEAD_DIM ** -0.5)
    att = attend(qf, k.astype(f32), v.astype(f32)).reshape(B, T, ATT_WIDTH).astype(x.dtype)
    xc, conv_buf = causal_conv(xr.astype(f32), conv_prev.astype(f32), conv_w.astype(f32), conv_b.astype(f32))
    hs, h_last = rg_lru(xc, h0.astype(f32), w_rgate, b_rgate, w_igate, b_igate, lru_lambda)
    rec = (jax.nn.gelu(yg.astype(f32)) * hs).astype(x.dtype)
    mixed = jnp.concatenate([rms_norm(att, g_att_out), rms_norm(rec, g_lru_out)], axis=-1) @ w_out
    x = x + rms_norm(mixed, g_post_mix)
    x = x + rms_norm(swiglu(rms_norm(x, g_pre_ffn), w_ffn_gate, w_ffn_up, w_ffn_down), g_post_ffn)
    return x, k, v, conv_buf.astype(x.dtype), h_last.astype(x.dtype)


def setup_inputs(seed: int = 0) -> dict:
    key = jax.random.key(seed)
    ks = jax.random.split(key, 24)
    f32 = jnp.float32

    def nrm(k, shape, scale):
        return scale * jax.random.normal(k, shape, f32)

    def gain(k, n):
        return 1.0 + nrm(k, (DEPTH, n), 0.02)

    w_buf = min(MAX_WINDOW, PAST_LEN)
    a0 = jax.random.uniform(ks[23], (DEPTH, LRU_WIDTH), f32, 0.9, 0.999)
    s = a0 ** (1.0 / LRU_C)
    lru_lambda = jnp.log(s) - jnp.log1p(-s)
    return {
        'x_prompt': nrm(ks[0], (BATCH, SEQ, D_MODEL), 1.0),
        'x_sample': nrm(ks[1], (DEC_BATCH, DEC_SEQ, D_MODEL), 1.0),
        'cache_k': nrm(ks[2], (DEPTH, DEC_BATCH, w_buf, N_HEADS, HEAD_DIM), 1.0),
        'cache_v': nrm(ks[3], (DEPTH, DEC_BATCH, w_buf, N_HEADS, HEAD_DIM), 1.0),
        'state_conv': nrm(ks[4], (DEPTH, DEC_BATCH, CONV_WIDTH - 1, LRU_WIDTH), 1.0),
        'state_h': nrm(ks[5], (DEPTH, DEC_BATCH, LRU_WIDTH), 0.5),
        'g_pre_mix': gain(ks[6], D_MODEL),
        'g_post_mix': gain(ks[7], D_MODEL),
        'w_in': nrm(ks[8], (DEPTH, D_MODEL, 3 * ATT_WIDTH + 2 * LRU_WIDTH), D_MODEL ** -0.5),
        'w_out': nrm(ks[9], (DEPTH, MIX_WIDTH, D_MODEL), MIX_WIDTH ** -0.5),
        'conv_w': nrm(ks[10], (DEPTH, CONV_WIDTH, LRU_WIDTH), CONV_WIDTH ** -0.5),
        'conv_b': nrm(ks[11], (DEPTH, LRU_WIDTH), 0.01),
        'w_rgate': nrm(ks[12], (DEPTH, LRU_BLOCKS, LRU_BLOCK, LRU_BLOCK), LRU_BLOCK ** -0.5),
        'b_rgate': nrm(ks[13], (DEPTH, LRU_WIDTH), 0.01),
        'w_igate': nrm(ks[14], (DEPTH, LRU_BLOCKS, LRU_BLOCK, LRU_BLOCK), LRU_BLOCK ** -0.5),
        'b_igate': nrm(ks[15], (DEPTH, LRU_WIDTH), 0.01),
        'lru_lambda': lru_lambda,
        'g_att_out': gain(ks[16], ATT_WIDTH),
        'g_lru_out': gain(ks[17], LRU_WIDTH),
        'g_pre_ffn': gain(ks[18], D_MODEL),
        'g_post_ffn': gain(ks[19], D_MODEL),
        'w_ffn_gate': nrm(ks[20], (DEPTH, D_MODEL, D_FF), D_MODEL ** -0.5),
        'w_ffn_up': nrm(ks[21], (DEPTH, D_MODEL, D_FF), D_MODEL ** -0.5),
        'w_ffn_down': nrm(ks[22], (DEPTH, D_FF, D_MODEL), D_FF ** -0.5),
    }


def reference(x_prompt, x_sample, cache_k, cache_v, state_conv, state_h,
              g_pre_mix, g_post_mix, w_in, w_out, conv_w, conv_b, w_rgate, b_rgate, w_igate, b_igate,
              lru_lambda, g_att_out, g_lru_out, g_pre_ffn, g_post_ffn, w_ffn_gate, w_ffn_up, w_ffn_down):
    layer_params = (g_pre_mix, g_post_mix, w_in, w_out, conv_w, conv_b, w_rgate, b_rgate, w_igate, b_igate,
                    lru_lambda, g_att_out, g_lru_out, g_pre_ffn, g_post_ffn, w_ffn_gate, w_ffn_up, w_ffn_down)
    b_p, s_p, _ = x_prompt.shape
    n_keep = min(MAX_WINDOW, s_p)
    conv0 = jnp.zeros((b_p, CONV_WIDTH - 1, LRU_WIDTH), x_prompt.dtype)
    h00 = jnp.zeros((b_p, LRU_WIDTH), x_prompt.dtype)
    y_p, y_s = x_prompt, x_sample
    kp, vp, cp, hp, ks, vs, cs, hs = [], [], [], [], [], [], [], []
    for l in range(DEPTH):
        lw = [p[l] for p in layer_params]
        y_p, k, v, c, h = decoder_layer(y_p, conv0, h00, attend_prompt, *lw)
        kp.append(k[:, -n_keep:])
        vp.append(v[:, -n_keep:])
        cp.append(c)
        hp.append(h)
        att_s = functools.partial(attend_sample, cache_k=cache_k[l], cache_v=cache_v[l])
        y_s, k, v, c, h = decoder_layer(y_s, state_conv[l], state_h[l], att_s, *lw)
        ks.append(k)
        vs.append(v)
        cs.append(c)
        hs.append(h)
    return (y_p, y_s, jnp.stack(kp), jnp.stack(vp), jnp.stack(cp), jnp.stack(hp),
            jnp.stack(ks), jnp.stack(vs), jnp.stack(cs), jnp.stack(hs))
```

```python
import functools

import jax
import jax.numpy as jnp
from jax import lax
from jax.experimental import pallas as pl
from jax.experimental.pallas import tpu as pltpu

F32 = jnp.float32
BF16 = jnp.bfloat16

N_HEADS = 8
HEAD_DIM = 64
ATT_WIDTH = N_HEADS * HEAD_DIM
LRU_BLOCKS = 8
CONV_WIDTH = 4
LRU_C = 8.0
DILATED = ((128, 1), (512, 4), (2048, 16))
MAX_WINDOW = 2048
BLK = 128
EPS = 1e-6
NEG = -1e30
Q_SCALE = HEAD_DIM ** -0.5

VMEM_LIMIT_BYTES = 56 * 1024 * 1024
FFN_CHUNKS = ((0, 1536), (1536, 2816))
SAMPLE_GROUP = 8
SCAN_CHUNK = 512


def _params(*sem):
    return pltpu.CompilerParams(dimension_semantics=sem, vmem_limit_bytes=VMEM_LIMIT_BYTES)


def _const_spec(shape, index):
    return pl.BlockSpec(shape, index, pipeline_mode=pl.Buffered(1))


def _rms(x, g):
    return x * lax.rsqrt(jnp.mean(x * x, axis=-1, keepdims=True) + EPS) * g


def _split_bf16(x):
    hi = x.astype(BF16)
    lo = (x - hi.astype(F32)).astype(BF16)
    return hi, lo


def _dot(a, b):
    return jnp.dot(a, b, preferred_element_type=F32)


def _dot_hilo(x, e):
    hi, lo = _split_bf16(x)
    return _dot(hi, e) + _dot(lo, e)


def _proj_body(x_ref, g_ref, w_ref, qkv_ref, p32_ref, *, col0):
    h = _rms(x_ref[...], g_ref[...]).astype(BF16)
    for c in range(5):
        lo, hi = c * 512, (c + 1) * 512
        pc = _dot(h, w_ref[:, lo:hi])
        if c == 0:
            qkv_ref[:, lo:hi] = (pc * Q_SCALE).astype(BF16)
        elif c < 3:
            qkv_ref[:, lo:hi] = pc.astype(BF16)
        if lo >= col0:
            p32_ref[:, lo - col0:hi - col0] = pc


def _proj(x, g, w, layer, tm, col0):
    m, d = x.shape
    n = w.shape[-1]
    return pl.pallas_call(
        functools.partial(_proj_body, col0=col0),
        grid=(m // tm,),
        in_specs=[
            pl.BlockSpec((tm, d), lambda i: (i, 0)),
            _const_spec((None, 1, d), lambda i: (layer, 0, 0)),
            _const_spec((None, d, n), lambda i: (layer, 0, 0)),
        ],
        out_specs=[
            pl.BlockSpec((tm, 3 * ATT_WIDTH), lambda i: (i, 0)),
            pl.BlockSpec((tm, n - col0), lambda i: (i, 0)),
        ],
        out_shape=[
            jax.ShapeDtypeStruct((m, 3 * ATT_WIDTH), BF16),
            jax.ShapeDtypeStruct((m, n - col0), F32),
        ],
        compiler_params=_params("parallel"),
        name="proj_in",
    )(x, g, w)


def _attn_body(q_ref, kp_ref, ko_ref, vp_ref, vo_ref, bp_ref, bo_ref, o_ref, lse_ref):
    n = pl.program_id(2)
    lane = lax.broadcasted_iota(jnp.int32, (1, 128), 1)
    lane_full = lax.broadcasted_iota(jnp.int32, (BLK, 128), 1)
    first_head = lane < HEAD_DIM
    nt = (((1,), (1,)), ((), ()))

    def run(with_prev):
        lse_tile = jnp.zeros((BLK, 128), F32)
        for pr in range(N_HEADS // 2):
            sl = slice(pr * 128, (pr + 1) * 128)
            q = q_ref[:, sl]
            ko = ko_ref[:, sl]
            vo = vo_ref[:, sl]
            outs = []
            for half in range(2):
                h = 2 * pr + half
                keep = first_head if half == 0 else jnp.logical_not(first_head)
                qm = jnp.where(keep, q, jnp.zeros_like(q))
                s_o = lax.dot_general(qm, ko, nt, preferred_element_type=F32) + bo_ref[h]
                m = jnp.max(s_o, axis=-1, keepdims=True)
                if with_prev:
                    s_p = lax.dot_general(qm, kp_ref[:, sl], nt, preferred_element_type=F32) + bp_ref[h]
                    m = jnp.maximum(m, jnp.max(s_p, axis=-1, keepdims=True))
                p_o = jnp.exp(s_o - m)
                l = jnp.sum(p_o, axis=-1, keepdims=True)
                acc = _dot(p_o.astype(BF16), vo)
                if with_prev:
                    p_p = jnp.exp(s_p - m)
                    l = l + jnp.sum(p_p, axis=-1, keepdims=True)
                    acc = acc + _dot(p_p.astype(BF16), vp_ref[:, sl])
                outs.append(acc / l)
                lse_tile = jnp.where(lane_full == h, m + jnp.log(l), lse_tile)
            o_ref[:, sl] = jnp.where(first_head, outs[0], outs[1]).astype(BF16)
        lse_ref[...] = lse_tile

    @pl.when(n == 0)
    def _():
        run(False)

    @pl.when(n > 0)
    def _():
        run(True)


def _attn_branch(qkv, bias_prev, bias_own, batch, seq, dil):
    n_u = seq // dil
    nb = n_u // BLK
    view = qkv.reshape(batch, n_u, dil * 3 * ATT_WIDTH)

    def cur(col):
        return pl.BlockSpec((None, BLK, ATT_WIDTH), lambda b, r, n: (b, n, 3 * r + col))

    def prev(col):
        return pl.BlockSpec((None, BLK, ATT_WIDTH), lambda b, r, n: (b, jnp.maximum(n - 1, 0), 3 * r + col))

    bias_spec = _const_spec((N_HEADS, BLK, BLK), lambda b, r, n: (0, 0, 0))
    o, lse = pl.pallas_call(
        _attn_body,
        grid=(batch, dil, nb),
        in_specs=[cur(0), prev(1), cur(1), prev(2), cur(2), bias_spec, bias_spec],
        out_specs=[
            pl.BlockSpec((None, BLK, ATT_WIDTH), lambda b, r, n: (b, n, r)),
            pl.BlockSpec((None, BLK, 128), lambda b, r, n: (b, n, r)),
        ],
        out_shape=[
            jax.ShapeDtypeStruct((batch, n_u, dil * ATT_WIDTH), BF16),
            jax.ShapeDtypeStruct((batch, n_u, dil * 128), F32),
        ],
        compiler_params=_params("parallel", "parallel", "arbitrary"),
        name=f"attn_d{dil}",
    )(view, view, view, view, view, bias_prev, bias_own)
    return o.reshape(batch * seq, ATT_WIDTH), lse.reshape(batch * seq, 128)


def _branch_bias(slopes, dil):
    i = jnp.arange(BLK)[:, None]
    j = jnp.arange(BLK)[None, :]
    out = []
    for steps in (i + BLK - j, i - j):
        ok = (steps >= 0) & (steps <= BLK)
        bias = -slopes[:, None, None] * (dil * steps).astype(F32)[None]
        out.append(jnp.where(ok[None], bias, NEG))
    return out


def _softplus(x):
    return jnp.maximum(x, 0.0) + jnp.log1p(jnp.exp(-jnp.abs(x)))


def _gelu(x):
    return 0.5 * x * (1.0 + jnp.tanh(0.7978845608028654 * (x + 0.044715 * (x * x * x))))


def _lru_terms(xc, wr_ref, br_ref, wi_ref, bi_ref, lam_ref):
    xb = xc.astype(BF16)
    r = jax.nn.sigmoid(_dot(xb, wr_ref[...]) + br_ref[...])
    i = jax.nn.sigmoid(_dot(xb, wi_ref[...]) + bi_ref[...])
    log_a = (-LRU_C * _softplus(-lam_ref[...])) * r
    a = jnp.exp(log_a)
    b = jnp.sqrt(-jnp.tanh(log_a) * (a * a + 1.0)) * (i * xc)
    return a, b


def _scan_body(xr_ref, yg_ref, cw_ref, cb_ref, wr_ref, br_ref, wi_ref, bi_ref, lam_ref,
               rec_ref, hl_ref, xbuf, a_s, b_s, h_s):
    tc = xr_ref.shape[0]

    @pl.when(pl.program_id(1) == 0)
    def _():
        xbuf[0:8, :] = jnp.zeros((8, xbuf.shape[1]), F32)
        h_s[...] = jnp.zeros_like(h_s)

    xbuf[8:, :] = xr_ref[...]
    xc = cb_ref[...]
    for j in range(CONV_WIDTH):
        xc = xc + xbuf[pl.ds(8 - (CONV_WIDTH - 1) + j, tc), :] * cw_ref[j:j + 1, :]
    xbuf[0:8, :] = xbuf[tc:tc + 8, :]

    a, b = _lru_terms(xc, wr_ref, br_ref, wi_ref, bi_ref, lam_ref)
    a_s[...] = a
    b_s[...] = b
    row = lax.broadcasted_iota(jnp.int32, (8, a.shape[1]), 0)

    def group(g, h):
        r0 = pl.multiple_of(g * 8, 8)
        ca = a_s[pl.ds(r0, 8), :]
        cb = b_s[pl.ds(r0, 8), :]
        for s in (1, 2, 4):
            keep = row >= s
            cb = jnp.where(keep, ca * pltpu.roll(cb, s, 0) + cb, cb)
            ca = jnp.where(keep, ca * pltpu.roll(ca, s, 0), ca)
        h8 = ca * h + cb
        b_s[pl.ds(r0, 8), :] = h8
        return h8[7:8, :]

    h = lax.fori_loop(0, tc // 8, group, h_s[...], unroll=8)
    h_s[...] = h
    hl_ref[...] = h
    rec_ref[...] = _gelu(yg_ref[...]) * b_s[...]


def _scan(p32, xr_col, lw, layer, batch, seq):
    tc = SCAN_CHUNK
    nc = seq // tc
    w = ATT_WIDTH
    view = p32.reshape(batch, seq, p32.shape[-1])

    def vec(name):
        return _const_spec((None, 1, w), lambda b, c: (layer, 0, 0)), lw[name]

    specs, args = [], []
    for spec, arr in (
        (pl.BlockSpec((None, tc, w), lambda b, c: (b, c, xr_col)), view),
        (pl.BlockSpec((None, tc, w), lambda b, c: (b, c, xr_col + 1)), view),
        (_const_spec((None, CONV_WIDTH, w), lambda b, c: (layer, 0, 0)), lw["conv_w"]),
        vec("conv_b"),
        (_const_spec((None, w, w), lambda b, c: (layer, 0, 0)), lw["w_rgate"]),
        vec("b_rgate"),
        (_const_spec((None, w, w), lambda b, c: (layer, 0, 0)), lw["w_igate"]),
        vec("b_igate"),
        vec("lru_lambda"),
    ):
        specs.append(spec)
        args.append(arr)
    rec, h_last = pl.pallas_call(
        _scan_body,
        grid=(batch, nc),
        in_specs=specs,
        out_specs=[
            pl.BlockSpec((None, tc, w), lambda b, c: (b, c, 0)),
            pl.BlockSpec((None, 1, w), lambda b, c: (b, 0, 0)),
        ],
        out_shape=[
            jax.ShapeDtypeStruct((batch, seq, w), F32),
            jax.ShapeDtypeStruct((batch, 1, w), F32),
        ],
        scratch_shapes=[
            pltpu.VMEM((tc + 8, w), F32),
            pltpu.VMEM((tc, w), F32),
            pltpu.VMEM((tc, w), F32),
            pltpu.VMEM((1, w), F32),
        ],
        compiler_params=_params("parallel", "arbitrary"),
        name="conv_rglru_scan",
    )(*args)
    return rec.reshape(batch * seq, w), h_last.reshape(batch, w)


def _step_body(xr_ref, yg_ref, sc_ref, h0_ref, cw_ref, cb_ref, wr_ref, br_ref, wi_ref, bi_ref, lam_ref,
               rec_ref, h_ref, conv_ref):
    w = xr_ref.shape[1]
    xr = xr_ref[...]
    xc = cb_ref[...] + xr * cw_ref[CONV_WIDTH - 1:CONV_WIDTH, :]
    for j in range(CONV_WIDTH - 1):
        xc = xc + sc_ref[:, j * w:(j + 1) * w] * cw_ref[j:j + 1, :]
    a, b = _lru_terms(xc, wr_ref, br_ref, wi_ref, bi_ref, lam_ref)
    h = a * h0_ref[...] + b
    h_ref[...] = h
    rec_ref[...] = _gelu(yg_ref[...]) * h
    conv_ref[:, 0:(CONV_WIDTH - 2) * w] = sc_ref[:, w:(CONV_WIDTH - 1) * w]
    conv_ref[:, (CONV_WIDTH - 2) * w:] = xr


def _step(p32, xr_col, state_conv, state_h, lw, layer):
    m = p32.shape[0]
    w = ATT_WIDTH
    sc = state_conv.reshape(state_conv.shape[0], m, (CONV_WIDTH - 1) * w)

    def vec(name):
        return _const_spec((None, 1, w), lambda i: (layer, 0, 0)), lw[name]

    specs, args = [], []
    for spec, arr in (
        (pl.BlockSpec((m, w), lambda i: (0, xr_col)), p32),
        (pl.BlockSpec((m, w), lambda i: (0, xr_col + 1)), p32),
        (pl.BlockSpec((None, m, (CONV_WIDTH - 1) * w), lambda i: (layer, 0, 0)), sc),
        (pl.BlockSpec((None, m, w), lambda i: (layer, 0, 0)), state_h),
        (_const_spec((None, CONV_WIDTH, w), lambda i: (layer, 0, 0)), lw["conv_w"]),
        vec("conv_b"),
        (_const_spec((None, w, w), lambda i: (layer, 0, 0)), lw["w_rgate"]),
        vec("b_rgate"),
        (_const_spec((None, w, w), lambda i: (layer, 0, 0)), lw["w_igate"]),
        vec("b_igate"),
        vec("lru_lambda"),
    ):
        specs.append(spec)
        args.append(arr)
    return pl.pallas_call(
        _step_body,
        grid=(1,),
        in_specs=specs,
        out_specs=[
            pl.BlockSpec((m, w), lambda i: (0, 0)),
            pl.BlockSpec((m, w), lambda i: (0, 0)),
            pl.BlockSpec((m, (CONV_WIDTH - 1) * w), lambda i: (0, 0)),
        ],
        out_shape=[
            jax.ShapeDtypeStruct((m, w), F32),
            jax.ShapeDtypeStruct((m, w), F32),
            jax.ShapeDtypeStruct((m, (CONV_WIDTH - 1) * w), F32),
        ],
        compiler_params=_params("arbitrary"),
        name="conv_rglru_step",
    )(*args)


def _sample_attn_body(q_ref, kn_ref, vn_ref, k1, k2, k3, v1, v2, v3, b1, b2, b3, et_ref, e_ref, o_ref):
    g = q_ref.shape[0]
    q_all = q_ref[...] * Q_SCALE
    et = et_ref[...]
    e = e_ref[...]
    n_branch = len(DILATED)
    s_new_all = _dot_hilo(q_all * kn_ref[...], et)

    p_new_rows, l_rows, acc_rows = [], [], []
    for bi in range(g):
        q = q_all[bi:bi + 1, :]
        s_new = s_new_all[bi:bi + 1, :]
        scores = []
        m = s_new
        for k_ref, b_ref in ((k1, b1), (k2, b2), (k3, b3)):
            s = _dot_hilo(k_ref[bi] * q, et) + b_ref[...]
            scores.append(s)
            m = jnp.maximum(m, jnp.max(s, axis=0, keepdims=True))
        p_new = jnp.exp(s_new - m) * float(n_branch)
        l = p_new
        acc = None
        for s, v_ref in zip(scores, (v1, v2, v3)):
            p = jnp.exp(s - m)
            l = l + jnp.sum(p, axis=0, keepdims=True)
            part = jnp.sum(_dot(p.astype(BF16), e) * v_ref[bi], axis=0, keepdims=True)
            acc = part if acc is None else acc + part
        p_new_rows.append(p_new)
        l_rows.append(l)
        acc_rows.append(acc)
    p_new = jnp.concatenate(p_new_rows, axis=0)
    acc = jnp.concatenate(acc_rows, axis=0) + _dot(p_new.astype(BF16), e) * vn_ref[...]
    o_ref[...] = acc / _dot_hilo(jnp.concatenate(l_rows, axis=0), e)


def _sample_attn(p32, cache_k, cache_v, biases, et, e, layer):
    depth, batch, w_buf = cache_k.shape[:3]
    w = ATT_WIDTH
    g = SAMPLE_GROUP
    assert w_buf >= MAX_WINDOW and w_buf % (BLK * DILATED[-1][1]) == 0

    def views(c):
        c = c.reshape(depth, batch, w_buf, w)
        out = []
        for _, dil in DILATED:
            n_u = w_buf // dil
            out.append((c.reshape(depth, batch, n_u, dil * w),
                        pl.BlockSpec((None, g, BLK, w), lambda i, nb=n_u // BLK - 1: (layer, i, nb, 0))))
        return out

    specs = [pl.BlockSpec((g, w), lambda i, c=c: (i, c)) for c in range(3)]
    args = [p32, p32, p32]
    for arr, spec in views(cache_k) + views(cache_v):
        specs.append(spec)
        args.append(arr)
    for b in biases:
        specs.append(_const_spec((BLK, 128), lambda i: (0, 0)))
        args.append(b)
    specs += [_const_spec((w, 128), lambda i: (0, 0)), _const_spec((128, w), lambda i: (0, 0))]
    args += [et, e]
    return pl.pallas_call(
        _sample_attn_body,
        grid=(batch // g,),
        in_specs=specs,
        out_specs=pl.BlockSpec((g, w), lambda i: (i, 0)),
        out_shape=jax.ShapeDtypeStruct((batch, w), F32),
        compiler_params=_params("parallel"),
        name="sample_attn",
    )(*args)


def _mix_body(*refs, n_branch):
    if n_branch:
        o_refs = refs[:n_branch]
        l_refs = refs[n_branch:2 * n_branch]
        e_ref = refs[2 * n_branch]
        rest = refs[2 * n_branch + 1:]
    else:
        att_ref = refs[0]
        rest = refs[1:]
    rec_ref, x_ref, ga_ref, gl_ref, w_ref, gp_ref, out_ref = rest

    if n_branch:
        lses = [r[...] for r in l_refs]
        top = functools.reduce(jnp.maximum, lses)
        e = e_ref[...]
        num = den = None
        for o_ref, lse in zip(o_refs, lses):
            we = _dot_hilo(jnp.exp(lse - top), e)
            t = we * o_ref[...].astype(F32)
            num = t if num is None else num + t
            den = we if den is None else den + we
        att = num / den
    else:
        att = att_ref[...]

    half = att.shape[1]
    att_n = _rms(att, ga_ref[...]).astype(BF16)
    rec_n = _rms(rec_ref[...], gl_ref[...]).astype(BF16)
    mixed = _dot(att_n, w_ref[0:half, :]) + _dot(rec_n, w_ref[half:, :])
    out_ref[...] = x_ref[...] + _rms(mixed, gp_ref[...])


def _mix(att_parts, rec, x, lw, layer, tm, e=None):
    m, d = x.shape
    w = ATT_WIDTH
    row = lambda width: pl.BlockSpec((tm, width), lambda i: (i, 0))
    specs, args = [], []
    if e is not None:
        os_, ls_ = zip(*att_parts)
        n_branch = len(os_)
        specs += [row(w)] * n_branch + [row(128)] * n_branch + [_const_spec((128, w), lambda i: (0, 0))]
        args += list(os_) + list(ls_) + [e]
    else:
        n_branch = 0
        specs.append(row(w))
        args.append(att_parts)
    specs += [
        row(w), row(d),
        _const_spec((None, 1, w), lambda i: (layer, 0, 0)),
        _const_spec((None, 1, w), lambda i: (layer, 0, 0)),
        _const_spec((None, 2 * w, d), lambda i: (layer, 0, 0)),
        _const_spec((None, 1, d), lambda i: (layer, 0, 0)),
    ]
    args += [rec, x, lw["g_att_out"], lw["g_lru_out"], lw["w_out"], lw["g_post_mix"]]
    return pl.pallas_call(
        functools.partial(_mix_body, n_branch=n_branch),
        grid=(m // tm,),
        in_specs=specs,
        out_specs=row(d),
        out_shape=jax.ShapeDtypeStruct((m, d), F32),
        compiler_params=_params("parallel"),
        name="mix_out",
    )(*args)


def _ffn_body(x_ref, g1_ref, wg_ref, wu_ref, wd_ref, g2_ref, out_ref):
    x = x_ref[...]
    h = _rms(x, g1_ref[...]).astype(BF16)
    acc = None
    for f0, f1 in FFN_CHUNKS:
        gate = _dot(h, wg_ref[:, f0:f1])
        up = _dot(h, wu_ref[:, f0:f1])
        act = (gate * jax.nn.sigmoid(gate) * up).astype(BF16)
        part = _dot(act, wd_ref[f0:f1, :])
        acc = part if acc is None else acc + part
    out_ref[...] = x + _rms(acc, g2_ref[...])


def _ffn(x, lw, layer, tm):
    m, d = x.shape
    f = lw["w_ffn_gate"].shape[-1]
    assert FFN_CHUNKS[-1][1] == f
    return pl.pallas_call(
        _ffn_body,
        grid=(m // tm,),
        in_specs=[
            pl.BlockSpec((tm, d), lambda i: (i, 0)),
            _const_spec((None, 1, d), lambda i: (layer, 0, 0)),
            _const_spec((None, d, f), lambda i: (layer, 0, 0)),
            _const_spec((None, d, f), lambda i: (layer, 0, 0)),
            _const_spec((None, f, d), lambda i: (layer, 0, 0)),
            _const_spec((None, 1, d), lambda i: (layer, 0, 0)),
        ],
        out_specs=pl.BlockSpec((tm, d), lambda i: (i, 0)),
        out_shape=jax.ShapeDtypeStruct((m, d), F32),
        compiler_params=_params("parallel"),
        name="ffn",
    )(x, lw["g_pre_ffn"], lw["w_ffn_gate"], lw["w_ffn_up"], lw["w_ffn_down"], lw["g_post_ffn"])


def _block_diag(w):
    depth, nblk, c, _ = w.shape
    eye = jnp.eye(nblk, dtype=w.dtype)
    return jnp.einsum("lncd,nm->lncmd", w, eye).reshape(depth, nblk * c, nblk * c)


def kernel(x_prompt, x_sample, cache_k, cache_v, state_conv, state_h, g_pre_mix, g_post_mix, w_in, w_out, conv_w, conv_b, w_rgate, b_rgate, w_igate, b_igate, lru_lambda, g_att_out, g_lru_out, g_pre_ffn, g_post_ffn, w_ffn_gate, w_ffn_up, w_ffn_down):
    batch, seq, d_model = x_prompt.shape
    dec_batch, dec_seq, _ = x_sample.shape
    depth = w_in.shape[0]
    w = ATT_WIDTH
    assert dec_seq == 1 and seq % (BLK * DILATED[-1][1]) == 0
    n_keep = min(MAX_WINDOW, seq)

    vec = lambda a: a.reshape(depth, 1, a.shape[-1])
    lw = dict(
        g_pre_mix=vec(g_pre_mix), g_post_mix=vec(g_post_mix), g_att_out=vec(g_att_out), g_lru_out=vec(g_lru_out),
        g_pre_ffn=vec(g_pre_ffn), g_post_ffn=vec(g_post_ffn), conv_b=vec(conv_b), b_rgate=vec(b_rgate),
        b_igate=vec(b_igate), lru_lambda=vec(lru_lambda), conv_w=conv_w,
        w_in=w_in.astype(BF16), w_out=w_out.astype(BF16),
        w_rgate=_block_diag(w_rgate).astype(BF16), w_igate=_block_diag(w_igate).astype(BF16),
        w_ffn_gate=w_ffn_gate.astype(BF16), w_ffn_up=w_ffn_up.astype(BF16), w_ffn_down=w_ffn_down.astype(BF16),
    )

    slopes = 2.0 ** (-(8.0 / N_HEADS) * jnp.arange(1, N_HEADS + 1, dtype=F32))
    prompt_bias = [_branch_bias(slopes, dil) for _, dil in DILATED]
    head_of_lane = jnp.arange(w) // HEAD_DIM
    e = (jnp.arange(128)[:, None] == head_of_lane[None, :]).astype(BF16)
    et = e.T
    steps = (BLK - jnp.arange(BLK)).astype(F32)[:, None]
    slope_lane = jnp.where(jnp.arange(128) < N_HEADS, jnp.pad(slopes, (0, 128 - N_HEADS)), 0.0)[None, :]
    sample_bias = [-slope_lane * (dil * steps) for _, dil in DILATED]

    tm_p = 512
    xp = x_prompt.reshape(batch * seq, d_model)
    xs = x_sample.reshape(dec_batch, d_model)
    kp, vp, cp, hp, ks, vs, cs, hs = [], [], [], [], [], [], [], []
    for l in range(depth):
        qkv, p32 = _proj(xp, lw["g_pre_mix"], lw["w_in"], l, tm_p, w)
        parts = [_attn_branch(qkv, bp, bo, batch, seq, dil) for (_, dil), (bp, bo) in zip(DILATED, prompt_bias)]
        rec, h_last = _scan(p32, 2, lw, l, batch, seq)
        xp = _mix(parts, rec, xp, lw, l, tm_p, e=e)
        xp = _ffn(xp, lw, l, tm_p)
        p3 = p32.reshape(batch, seq, 4 * w)
        kp.append(p3[:, seq - n_keep:, 0:w].reshape(batch, n_keep, N_HEADS, HEAD_DIM))
        vp.append(p3[:, seq - n_keep:, w:2 * w].reshape(batch, n_keep, N_HEADS, HEAD_DIM))
        cp.append(p3[:, seq - (CONV_WIDTH - 1):, 2 * w:3 * w])
        hp.append(h_last)
        _, s32 = _proj(xs, lw["g_pre_mix"], lw["w_in"], l, dec_batch, 0)
        att = _sample_attn(s32, cache_k, cache_v, sample_bias, et, e, l)
        rec_s, h_s, conv_s = _step(s32, 3, state_conv, state_h, lw, l)
        xs = _mix(att, rec_s, xs, lw, l, dec_batch)
        xs = _ffn(xs, lw, l, dec_batch)
        ks.append(s32[:, w:2 * w].reshape(dec_batch, 1, N_HEADS, HEAD_DIM))
        vs.append(s32[:, 2 * w:3 * w].reshape(dec_batch, 1, N_HEADS, HEAD_DIM))
        cs.append(conv_s.reshape(dec_batch, CONV_WIDTH - 1, w))
        hs.append(h_s)
    return (xp.reshape(batch, seq, d_model), xs.reshape(dec_batch, 1, d_model),
            jnp.stack(kp), jnp.stack(vp), jnp.stack(cp), jnp.stack(hp),
            jnp.stack(ks), jnp.stack(vs), jnp.stack(cs), jnp.stack(hs))
```

```python
import functools

import jax
import jax.numpy as jnp
from jax import lax
from jax.experimental import pallas as pl
from jax.experimental.pallas import tpu as pltpu

F32 = jnp.float32
BF16 = jnp.bfloat16

N_HEADS = 8
HEAD_DIM = 64
ATT_WIDTH = N_HEADS * HEAD_DIM
LRU_BLOCKS = 8
CONV_WIDTH = 4
LRU_C = 8.0
DILATED = ((128, 1), (512, 4), (2048, 16))
MAX_WINDOW = 2048
BLK = 128
EPS = 1e-6
NEG = -1e30
Q_SCALE = HEAD_DIM ** -0.5

VMEM_LIMIT_BYTES = 56 * 1024 * 1024
FFN_CHUNKS = ((0, 1536), (1536, 2816))
SAMPLE_GROUP = 4
SCAN_CHUNK = 512


def _params(*sem):
    return pltpu.CompilerParams(dimension_semantics=sem, vmem_limit_bytes=VMEM_LIMIT_BYTES)


def _const_spec(shape, index):
    return pl.BlockSpec(shape, index, pipeline_mode=pl.Buffered(1))


def _rms(x, g):
    return x * lax.rsqrt(jnp.mean(x * x, axis=-1, keepdims=True) + EPS) * g


def _split_bf16(x):
    hi = x.astype(BF16)
    lo = (x - hi.astype(F32)).astype(BF16)
    return hi, lo


def _dot(a, b):
    return jnp.dot(a, b, preferred_element_type=F32)


def _dot_hilo(x, e):
    hi, lo = _split_bf16(x)
    return _dot(hi, e) + _dot(lo, e)


def _proj_body(x_ref, g_ref, w_ref, qkv_ref, p32_ref, *, col0):
    h = _rms(x_ref[...], g_ref[...]).astype(BF16)
    for c in range(5):
        lo, hi = c * 512, (c + 1) * 512
        pc = _dot(h, w_ref[:, lo:hi])
        if c == 0:
            qkv_ref[:, lo:hi] = (pc * Q_SCALE).astype(BF16)
        elif c < 3:
            qkv_ref[:, lo:hi] = pc.astype(BF16)
        if lo >= col0:
            p32_ref[:, lo - col0:hi - col0] = pc


def _proj(x, g, w, layer, tm, col0):
    m, d = x.shape
    n = w.shape[-1]
    return pl.pallas_call(
        functools.partial(_proj_body, col0=col0),
        grid=(m // tm,),
        in_specs=[
            pl.BlockSpec((tm, d), lambda i: (i, 0)),
            _const_spec((None, 1, d), lambda i: (layer, 0, 0)),
            _const_spec((None, d, n), lambda i: (layer, 0, 0)),
        ],
        out_specs=[
            pl.BlockSpec((tm, 3 * ATT_WIDTH), lambda i: (i, 0)),
            pl.BlockSpec((tm, n - col0), lambda i: (i, 0)),
        ],
        out_shape=[
            jax.ShapeDtypeStruct((m, 3 * ATT_WIDTH), BF16),
            jax.ShapeDtypeStruct((m, n - col0), F32),
        ],
        compiler_params=_params("parallel"),
        name="proj_in",
    )(x, g, w)


def _attn_body(q_ref, kp_ref, ko_ref, vp_ref, vo_ref, bp_ref, bo_ref, o_ref, lse_ref):
    n = pl.program_id(2)
    lane = lax.broadcasted_iota(jnp.int32, (1, 128), 1)
    lane_full = lax.broadcasted_iota(jnp.int32, (BLK, 128), 1)
    first_head = lane < HEAD_DIM
    nt = (((1,), (1,)), ((), ()))

    def run(with_prev):
        lse_tile = jnp.zeros((BLK, 128), F32)
        for pr in range(N_HEADS // 2):
            sl = slice(pr * 128, (pr + 1) * 128)
            q = q_ref[:, sl]
            ko = ko_ref[:, sl]
            vo = vo_ref[:, sl]
            outs = []
            for half in range(2):
                h = 2 * pr + half
                keep = first_head if half == 0 else jnp.logical_not(first_head)
                qm = jnp.where(keep, q, jnp.zeros_like(q))
                s_o = lax.dot_general(qm, ko, nt, preferred_element_type=F32) + bo_ref[h]
                m = jnp.max(s_o, axis=-1, keepdims=True)
                if with_prev:
                    s_p = lax.dot_general(qm, kp_ref[:, sl], nt, preferred_element_type=F32) + bp_ref[h]
                    m = jnp.maximum(m, jnp.max(s_p, axis=-1, keepdims=True))
                p_o = jnp.exp(s_o - m)
                l = jnp.sum(p_o, axis=-1, keepdims=True)
                acc = _dot(p_o.astype(BF16), vo)
                if with_prev:
                    p_p = jnp.exp(s_p - m)
                    l = l + jnp.sum(p_p, axis=-1, keepdims=True)
                    acc = acc + _dot(p_p.astype(BF16), vp_ref[:, sl])
                outs.append(acc / l)
                lse_tile = jnp.where(lane_full == h, m + jnp.log(l), lse_tile)
            o_ref[:, sl] = jnp.where(first_head, outs[0], outs[1]).astype(BF16)
        lse_ref[...] = lse_tile

    @pl.when(n == 0)
    def _():
        run(False)

    @pl.when(n > 0)
    def _():
        run(True)


def _attn_branch(qkv, bias_prev, bias_own, batch, seq, dil):
    n_u = seq // dil
    nb = n_u // BLK
    view = qkv.reshape(batch, n_u, dil * 3 * ATT_WIDTH)

    def cur(col):
        return pl.BlockSpec((None, BLK, ATT_WIDTH), lambda b, r, n: (b, n, 3 * r + col))

    def prev(col):
        return pl.BlockSpec((None, BLK, ATT_WIDTH), lambda b, r, n: (b, jnp.maximum(n - 1, 0), 3 * r + col))

    bias_spec = _const_spec((N_HEADS, BLK, BLK), lambda b, r, n: (0, 0, 0))
    o, lse = pl.pallas_call(
        _attn_body,
        grid=(batch, dil, nb),
        in_specs=[cur(0), prev(1), cur(1), prev(2), cur(2), bias_spec, bias_spec],
        out_specs=[
            pl.BlockSpec((None, BLK, ATT_WIDTH), lambda b, r, n: (b, n, r)),
            pl.BlockSpec((None, BLK, 128), lambda b, r, n: (b, n, r)),
        ],
        out_shape=[
            jax.ShapeDtypeStruct((batch, n_u, dil * ATT_WIDTH), BF16),
            jax.ShapeDtypeStruct((batch, n_u, dil * 128), F32),
        ],
        compiler_params=_params("parallel", "parallel", "arbitrary"),
        name=f"attn_d{dil}",
    )(view, view, view, view, view, bias_prev, bias_own)
    return o.reshape(batch * seq, ATT_WIDTH), lse.reshape(batch * seq, 128)


def _branch_bias(slopes, dil):
    i = jnp.arange(BLK)[:, None]
    j = jnp.arange(BLK)[None, :]
    out = []
    for steps in (i + BLK - j, i - j):
        ok = (steps >= 0) & (steps <= BLK)
        bias = -slopes[:, None, None] * (dil * steps).astype(F32)[None]
        out.append(jnp.where(ok[None], bias, NEG))
    return out


def _softplus(x):
    return jnp.maximum(x, 0.0) + jnp.log1p(jnp.exp(-jnp.abs(x)))


def _gelu(x):
    return 0.5 * x * (1.0 + jnp.tanh(0.7978845608028654 * (x + 0.044715 * (x * x * x))))


def _lru_terms(xc, wr_ref, br_ref, wi_ref, bi_ref, lam_ref):
    xb = xc.astype(BF16)
    r = jax.nn.sigmoid(_dot(xb, wr_ref[...]) + br_ref[...])
    i = jax.nn.sigmoid(_dot(xb, wi_ref[...]) + bi_ref[...])
    log_a = (-LRU_C * _softplus(-lam_ref[...])) * r
    a = jnp.exp(log_a)
    b = jnp.sqrt(-jnp.tanh(log_a) * (a * a + 1.0)) * (i * xc)
    return a, b


def _scan_body(xr_ref, yg_ref, cw_ref, cb_ref, wr_ref, br_ref, wi_ref, bi_ref, lam_ref,
               rec_ref, hl_ref, xbuf, a_s, b_s, h_s):
    tc = xr_ref.shape[0]

    @pl.when(pl.program_id(1) == 0)
    def _():
        xbuf[0:8, :] = jnp.zeros((8, xbuf.shape[1]), F32)
        h_s[...] = jnp.zeros_like(h_s)

    xbuf[8:, :] = xr_ref[...]
    xc = cb_ref[...]
    for j in range(CONV_WIDTH):
        xc = xc + xbuf[pl.ds(8 - (CONV_WIDTH - 1) + j, tc), :] * cw_ref[j:j + 1, :]
    xbuf[0:8, :] = xbuf[tc:tc + 8, :]

    a, b = _lru_terms(xc, wr_ref, br_ref, wi_ref, bi_ref, lam_ref)
    a_s[...] = a
    b_s[...] = b
    row = lax.broadcasted_iota(jnp.int32, (8, a.shape[1]), 0)

    def group(g, h):
        r0 = pl.multiple_of(g * 8, 8)
        ca = a_s[pl.ds(r0, 8), :]
        cb = b_s[pl.ds(r0, 8), :]
        for s in (1, 2, 4):
            keep = row >= s
            cb = jnp.where(keep, ca * pltpu.roll(cb, s, 0) + cb, cb)
            ca = jnp.where(keep, ca * pltpu.roll(ca, s, 0), ca)
        h8 = ca * h + cb
        b_s[pl.ds(r0, 8), :] = h8
        return h8[7:8, :]

    h = lax.fori_loop(0, tc // 8, group, h_s[...], unroll=8)
    h_s[...] = h
    hl_ref[...] = h
    rec_ref[...] = _gelu(yg_ref[...]) * b_s[...]


def _scan(p32, xr_col, lw, layer, batch, seq):
    tc = SCAN_CHUNK
    nc = seq // tc
    w = ATT_WIDTH
    view = p32.reshape(batch, seq, p32.shape[-1])

    def vec(name):
        return _const_spec((None, 1, w), lambda b, c: (layer, 0, 0)), lw[name]

    specs, args = [], []
    for spec, arr in (
        (pl.BlockSpec((None, tc, w), lambda b, c: (b, c, xr_col)), view),
        (pl.BlockSpec((None, tc, w), lambda b, c: (b, c, xr_col + 1)), view),
        (_const_spec((None, CONV_WIDTH, w), lambda b, c: (layer, 0, 0)), lw["conv_w"]),
        vec("conv_b"),
        (_const_spec((None, w, w), lambda b, c: (layer, 0, 0)), lw["w_rgate"]),
        vec("b_rgate"),
        (_const_spec((None, w, w), lambda b, c: (layer, 0, 0)), lw["w_igate"]),
        vec("b_igate"),
        vec("lru_lambda"),
    ):
        specs.append(spec)
        args.append(arr)
    rec, h_last = pl.pallas_call(
        _scan_body,
        grid=(batch, nc),
        in_specs=specs,
        out_specs=[
            pl.BlockSpec((None, tc, w), lambda b, c: (b, c, 0)),
            pl.BlockSpec((None, 1, w), lambda b, c: (b, 0, 0)),
        ],
        out_shape=[
            jax.ShapeDtypeStruct((batch, seq, w), F32),
            jax.ShapeDtypeStruct((batch, 1, w), F32),
        ],
        scratch_shapes=[
            pltpu.VMEM((tc + 8, w), F32),
            pltpu.VMEM((tc, w), F32),
            pltpu.VMEM((tc, w), F32),
            pltpu.VMEM((1, w), F32),
        ],
        compiler_params=_params("parallel", "arbitrary"),
        name="conv_rglru_scan",
    )(*args)
    return rec.reshape(batch * seq, w), h_last.reshape(batch, w)


def _step_body(xr_ref, yg_ref, sc_ref, h0_ref, cw_ref, cb_ref, wr_ref, br_ref, wi_ref, bi_ref, lam_ref,
               rec_ref, h_ref, conv_ref):
    w = xr_ref.shape[1]
    xr = xr_ref[...]
    xc = cb_ref[...] + xr * cw_ref[CONV_WIDTH - 1:CONV_WIDTH, :]
    for j in range(CONV_WIDTH - 1):
        xc = xc + sc_ref[:, j * w:(j + 1) * w] * cw_ref[j:j + 1, :]
    a, b = _lru_terms(xc, wr_ref, br_ref, wi_ref, bi_ref, lam_ref)
    h = a * h0_ref[...] + b
    h_ref[...] = h
    rec_ref[...] = _gelu(yg_ref[...]) * h
    conv_ref[:, 0:(CONV_WIDTH - 2) * w] = sc_ref[:, w:(CONV_WIDTH - 1) * w]
    conv_ref[:, (CONV_WIDTH - 2) * w:] = xr


def _step(p32, xr_col, state_conv, state_h, lw, layer):
    m = p32.shape[0]
    w = ATT_WIDTH
    sc = state_conv.reshape(state_conv.shape[0], m, (CONV_WIDTH - 1) * w)

    def vec(name):
        return _const_spec((None, 1, w), lambda i: (layer, 0, 0)), lw[name]

    specs, args = [], []
    for spec, arr in (
        (pl.BlockSpec((m, w), lambda i: (0, xr_col)), p32),
        (pl.BlockSpec((m, w), lambda i: (0, xr_col + 1)), p32),
        (pl.BlockSpec((None, m, (CONV_WIDTH - 1) * w), lambda i: (layer, 0, 0)), sc),
        (pl.BlockSpec((None, m, w), lambda i: (layer, 0, 0)), state_h),
        (_const_spec((None, CONV_WIDTH, w), lambda i: (layer, 0, 0)), lw["conv_w"]),
        vec("conv_b"),
        (_const_spec((None, w, w), lambda i: (layer, 0, 0)), lw["w_rgate"]),
        vec("b_rgate"),
        (_const_spec((None, w, w), lambda i: (layer, 0, 0)), lw["w_igate"]),
        vec("b_igate"),
        vec("lru_lambda"),
    ):
        specs.append(spec)
        args.append(arr)
    return pl.pallas_call(
        _step_body,
        grid=(1,),
        in_specs=specs,
        out_specs=[
            pl.BlockSpec((m, w), lambda i: (0, 0)),
            pl.BlockSpec((m, w), lambda i: (0, 0)),
            pl.BlockSpec((m, (CONV_WIDTH - 1) * w), lambda i: (0, 0)),
        ],
        out_shape=[
            jax.ShapeDtypeStruct((m, w), F32),
            jax.ShapeDtypeStruct((m, w), F32),
            jax.ShapeDtypeStruct((m, (CONV_WIDTH - 1) * w), F32),
        ],
        compiler_params=_params("arbitrary"),
        name="conv_rglru_step",
    )(*args)


def _sample_attn_body(q_ref, kn_ref, vn_ref, k1, k2, k3, v1, v2, v3, slope_ref, o_ref):
    g = q_ref.shape[0]
    n_branch = len(DILATED)
    steps = (BLK - lax.broadcasted_iota(jnp.int32, (BLK, N_HEADS, 1), 0)).astype(F32)
    biases = [steps * (slope_ref[...] * -float(dil))[None] for _, dil in DILATED]

    for bi in range(g):
        q = q_ref[bi] * Q_SCALE
        s_new = jnp.sum(q * kn_ref[bi], axis=-1, keepdims=True)
        scores = []
        m = s_new
        for k_ref, bias in zip((k1, k2, k3), biases):
            s = jnp.sum(k_ref[bi] * q[None], axis=-1, keepdims=True) + bias
            scores.append(s)
            m = jnp.maximum(m, jnp.max(s, axis=0))
        p_new = jnp.exp(s_new - m) * float(n_branch)
        l = p_new
        acc = p_new * vn_ref[bi]
        for s, v_ref in zip(scores, (v1, v2, v3)):
            p = jnp.exp(s - m[None])
            l = l + jnp.sum(p, axis=0)
            acc = acc + jnp.sum(p * v_ref[bi], axis=0)
        o_ref[bi] = acc / l


def _sample_attn(p32, cache_k, cache_v, slopes, layer):
    depth, batch, w_buf, nh, hd = cache_k.shape
    w = nh * hd
    g = SAMPLE_GROUP
    assert w_buf >= MAX_WINDOW and w_buf % (BLK * DILATED[-1][1]) == 0

    def views(c):
        out = []
        for _, dil in DILATED:
            n_u = w_buf // dil
            out.append((c.reshape(depth, batch, n_u, dil, nh, hd),
                        pl.BlockSpec((None, g, BLK, None, nh, hd),
                                     lambda i, nb=n_u // BLK - 1: (layer, i, nb, 0, 0, 0))))
        return out

    row = pl.BlockSpec((g, nh, hd), lambda i: (i, 0, 0))
    specs = [row] * 3
    args = [p32[:, c * w:(c + 1) * w].reshape(batch, nh, hd) for c in range(3)]
    for arr, spec in views(cache_k) + views(cache_v):
        specs.append(spec)
        args.append(arr)
    specs.append(_const_spec((nh, 1), lambda i: (0, 0)))
    args.append(slopes.reshape(nh, 1))
    out = pl.pallas_call(
        _sample_attn_body,
        grid=(batch // g,),
        in_specs=specs,
        out_specs=row,
        out_shape=jax.ShapeDtypeStruct((batch, nh, hd), F32),
        compiler_params=_params("parallel"),
        name="sample_attn",
    )(*args)
    return out.reshape(batch, w)


def _mix_body(*refs, n_branch):
    if n_branch:
        o_refs = refs[:n_branch]
        l_refs = refs[n_branch:2 * n_branch]
        e_ref = refs[2 * n_branch]
        rest = refs[2 * n_branch + 1:]
    else:
        att_ref = refs[0]
        rest = refs[1:]
    rec_ref, x_ref, ga_ref, gl_ref, w_ref, gp_ref, out_ref = rest

    if n_branch:
        lses = [r[...] for r in l_refs]
        top = functools.reduce(jnp.maximum, lses)
        e = e_ref[...]
        num = den = None
        for o_ref, lse in zip(o_refs, lses):
            we = _dot_hilo(jnp.exp(lse - top), e)
            t = we * o_ref[...].astype(F32)
            num = t if num is None else num + t
            den = we if den is None else den + we
        att = num / den
    else:
        att = att_ref[...]

    half = att.shape[1]
    att_n = _rms(att, ga_ref[...]).astype(BF16)
    rec_n = _rms(rec_ref[...], gl_ref[...]).astype(BF16)
    mixed = _dot(att_n, w_ref[0:half, :]) + _dot(rec_n, w_ref[half:, :])
    out_ref[...] = x_ref[...] + _rms(mixed, gp_ref[...])


def _mix(att_parts, rec, x, lw, layer, tm, e=None):
    m, d = x.shape
    w = ATT_WIDTH
    row = lambda width: pl.BlockSpec((tm, width), lambda i: (i, 0))
    specs, args = [], []
    if e is not None:
        os_, ls_ = zip(*att_parts)
        n_branch = len(os_)
        specs += [row(w)] * n_branch + [row(128)] * n_branch + [_const_spec((128, w), lambda i: (0, 0))]
        args += list(os_) + list(ls_) + [e]
    else:
        n_branch = 0
        specs.append(row(w))
        args.append(att_parts)
    specs += [
        row(w), row(d),
        _const_spec((None, 1, w), lambda i: (layer, 0, 0)),
        _const_spec((None, 1, w), lambda i: (layer, 0, 0)),
        _const_spec((None, 2 * w, d), lambda i: (layer, 0, 0)),
        _const_spec((None, 1, d), lambda i: (layer, 0, 0)),
    ]
    args += [rec, x, lw["g_att_out"], lw["g_lru_out"], lw["w_out"], lw["g_post_mix"]]
    return pl.pallas_call(
        functools.partial(_mix_body, n_branch=n_branch),
        grid=(m // tm,),
        in_specs=specs,
        out_specs=row(d),
        out_shape=jax.ShapeDtypeStruct((m, d), F32),
        compiler_params=_params("parallel"),
        name="mix_out",
    )(*args)


def _ffn_body(x_ref, g1_ref, wg_ref, wu_ref, wd_ref, g2_ref, out_ref):
    x = x_ref[...]
    h = _rms(x, g1_ref[...]).astype(BF16)
    acc = None
    for f0, f1 in FFN_CHUNKS:
        gate = _dot(h, wg_ref[:, f0:f1])
        up = _dot(h, wu_ref[:, f0:f1])
        act = (gate * jax.nn.sigmoid(gate) * up).astype(BF16)
        part = _dot(act, wd_ref[f0:f1, :])
        acc = part if acc is None else acc + part
    out_ref[...] = x + _rms(acc, g2_ref[...])


def _ffn(x, lw, layer, tm):
    m, d = x.shape
    f = lw["w_ffn_gate"].shape[-1]
    assert FFN_CHUNKS[-1][1] == f
    return pl.pallas_call(
        _ffn_body,
        grid=(m // tm,),
        in_specs=[
            pl.BlockSpec((tm, d), lambda i: (i, 0)),
            _const_spec((None, 1, d), lambda i: (layer, 0, 0)),
            _const_spec((None, d, f), lambda i: (layer, 0, 0)),
            _const_spec((None, d, f), lambda i: (layer, 0, 0)),
            _const_spec((None, f, d), lambda i: (layer, 0, 0)),
            _const_spec((None, 1, d), lambda i: (layer, 0, 0)),
        ],
        out_specs=pl.BlockSpec((tm, d), lambda i: (i, 0)),
        out_shape=jax.ShapeDtypeStruct((m, d), F32),
        compiler_params=_params("parallel"),
        name="ffn",
    )(x, lw["g_pre_ffn"], lw["w_ffn_gate"], lw["w_ffn_up"], lw["w_ffn_down"], lw["g_post_ffn"])


def _block_diag(w):
    depth, nblk, c, _ = w.shape
    eye = jnp.eye(nblk, dtype=w.dtype)
    return jnp.einsum("lncd,nm->lncmd", w, eye).reshape(depth, nblk * c, nblk * c)


def kernel(x_prompt, x_sample, cache_k, cache_v, state_conv, state_h, g_pre_mix, g_post_mix, w_in, w_out, conv_w, conv_b, w_rgate, b_rgate, w_igate, b_igate, lru_lambda, g_att_out, g_lru_out, g_pre_ffn, g_post_ffn, w_ffn_gate, w_ffn_up, w_ffn_down):
    batch, seq, d_model = x_prompt.shape
    dec_batch, dec_seq, _ = x_sample.shape
    depth = w_in.shape[0]
    w = ATT_WIDTH
    assert dec_seq == 1 and seq % (BLK * DILATED[-1][1]) == 0
    n_keep = min(MAX_WINDOW, seq)

    vec = lambda a: a.reshape(depth, 1, a.shape[-1])
    lw = dict(
        g_pre_mix=vec(g_pre_mix), g_post_mix=vec(g_post_mix), g_att_out=vec(g_att_out), g_lru_out=vec(g_lru_out),
        g_pre_ffn=vec(g_pre_ffn), g_post_ffn=vec(g_post_ffn), conv_b=vec(conv_b), b_rgate=vec(b_rgate),
        b_igate=vec(b_igate), lru_lambda=vec(lru_lambda), conv_w=conv_w,
        w_in=w_in.astype(BF16), w_out=w_out.astype(BF16),
        w_rgate=_block_diag(w_rgate).astype(BF16), w_igate=_block_diag(w_igate).astype(BF16),
        w_ffn_gate=w_ffn_gate.astype(BF16), w_ffn_up=w_ffn_up.astype(BF16), w_ffn_down=w_ffn_down.astype(BF16),
    )

    slopes = 2.0 ** (-(8.0 / N_HEADS) * jnp.arange(1, N_HEADS + 1, dtype=F32))
    prompt_bias = [_branch_bias(slopes, dil) for _, dil in DILATED]
    head_of_lane = jnp.arange(w) // HEAD_DIM
    e = (jnp.arange(128)[:, None] == head_of_lane[None, :]).astype(BF16)

    tm_p = 512
    xp = x_prompt.reshape(batch * seq, d_model)
    xs = x_sample.reshape(dec_batch, d_model)
    kp, vp, cp, hp, ks, vs, cs, hs = [], [], [], [], [], [], [], []
    for l in range(depth):
        qkv, p32 = _proj(xp, lw["g_pre_mix"], lw["w_in"], l, tm_p, w)
        parts = [_attn_branch(qkv, bp, bo, batch, seq, dil) for (_, dil), (bp, bo) in zip(DILATED, prompt_bias)]
        rec, h_last = _scan(p32, 2, lw, l, batch, seq)
        xp = _mix(parts, rec, xp, lw, l, tm_p, e=e)
        xp = _ffn(xp, lw, l, tm_p)
        p3 = p32.reshape(batch, seq, 4 * w)
        kp.append(p3[:, seq - n_keep:, 0:w].reshape(batch, n_keep, N_HEADS, HEAD_DIM))
        vp.append(p3[:, seq - n_keep:, w:2 * w].reshape(batch, n_keep, N_HEADS, HEAD_DIM))
        cp.append(p3[:, seq - (CONV_WIDTH - 1):, 2 * w:3 * w])
        hp.append(h_last)
        _, s32 = _proj(xs, lw["g_pre_mix"], lw["w_in"], l, dec_batch, 0)
        att = _sample_attn(s32, cache_k, cache_v, slopes, l)
        rec_s, h_s, conv_s = _step(s32, 3, state_conv, state_h, lw, l)
        xs = _mix(att, rec_s, xs, lw, l, dec_batch)
        xs = _ffn(xs, lw, l, dec_batch)
        ks.append(s32[:, w:2 * w].reshape(dec_batch, 1, N_HEADS, HEAD_DIM))
        vs.append(s32[:, 2 * w:3 * w].reshape(dec_batch, 1, N_HEADS, HEAD_DIM))
        cs.append(conv_s.reshape(dec_batch, CONV_WIDTH - 1, w))
        hs.append(h_s)
    return (xp.reshape(batch, seq, d_model), xs.reshape(dec_batch, 1, d_model),
            jnp.stack(kp), jnp.stack(vp), jnp.stack(cp), jnp.stack(hp),
            jnp.stack(ks), jnp.stack(vs), jnp.stack(cs), jnp.stack(hs))
```

```python
import functools

import jax
import jax.numpy as jnp
from jax import lax
from jax.experimental import pallas as pl
from jax.experimental.pallas import tpu as pltpu

F32 = jnp.float32
BF16 = jnp.bfloat16

N_HEADS = 8
HEAD_DIM = 64
ATT_WIDTH = N_HEADS * HEAD_DIM
N_PAIRS = N_HEADS // 2
CONV_WIDTH = 4
LRU_C = 8.0
DILATED = ((128, 1), (512, 4), (2048, 16))
MAX_WINDOW = 2048
BLK = 128
EPS = 1e-6
NEG = -1e30
Q_SCALE = HEAD_DIM ** -0.5

VMEM_LIMIT_BYTES = 56 * 1024 * 1024
FFN_CHUNKS = ((0, 1536), (1536, 2816))
SCAN_CHUNK = 512
ATT_CHUNK = BLK * DILATED[-1][1]
ATT_GROUP = 4


def _params(*sem):
    return pltpu.CompilerParams(dimension_semantics=sem, vmem_limit_bytes=VMEM_LIMIT_BYTES)


def _const_spec(shape, index):
    return pl.BlockSpec(shape, index, pipeline_mode=pl.Buffered(1))


def _rms(x, g):
    return x * lax.rsqrt(jnp.mean(x * x, axis=-1, keepdims=True) + EPS) * g


def _dot(a, b):
    return jnp.dot(a, b, preferred_element_type=F32)


def _proj_body(x_ref, g_ref, w_ref, *out_refs, slabs):
    h = _rms(x_ref[...], g_ref[...]).astype(BF16)
    n = w_ref.shape[1]
    n_att = 3 * ATT_WIDTH
    for c in range(n // ATT_WIDTH):
        lo, hi = c * ATT_WIDTH, (c + 1) * ATT_WIDTH
        pc = _dot(h, w_ref[:, lo:hi])
        if not slabs:
            out_refs[0][:, lo:hi] = pc
        elif lo >= n_att:
            out_refs[1][:, lo - n_att:hi - n_att] = pc
        else:
            if c == 0:
                pc = pc * Q_SCALE
            for p in range(N_PAIRS):
                out_refs[0][c * N_PAIRS + p] = pc[:, p * 128:(p + 1) * 128]


def _proj(x, g, w, layer, tm, slabs):
    m, d = x.shape
    n = w.shape[-1]
    if slabs:
        n_rest = n - 3 * ATT_WIDTH
        out_specs = [pl.BlockSpec((3 * N_PAIRS, tm, 128), lambda i: (0, i, 0)),
                     pl.BlockSpec((tm, n_rest), lambda i: (i, 0))]
        out_shape = [jax.ShapeDtypeStruct((3 * N_PAIRS, m, 128), F32),
                     jax.ShapeDtypeStruct((m, n_rest), F32)]
    else:
        out_specs = [pl.BlockSpec((tm, n), lambda i: (i, 0))]
        out_shape = [jax.ShapeDtypeStruct((m, n), F32)]
    return pl.pallas_call(
        functools.partial(_proj_body, slabs=slabs),
        grid=(m // tm,),
        in_specs=[
            pl.BlockSpec((tm, d), lambda i: (i, 0)),
            _const_spec((None, 1, d), lambda i: (layer, 0, 0)),
            _const_spec((None, d, n), lambda i: (layer, 0, 0)),
        ],
        out_specs=out_specs,
        out_shape=out_shape,
        compiler_params=_params("parallel"),
        name="proj_in",
    )(x, g, w)


def _attn_body(q_ref, k_ref, v_ref, bias_ref, o_ref, kk, vv, acc_s, m_s, l_s):
    c = pl.program_id(2)
    C = q_ref.shape[0]
    nt = (((1,), (1,)), ((), ()))
    first_head = lax.broadcasted_iota(jnp.int32, (1, 128), 1) < HEAD_DIM

    @pl.when(c == 0)
    def _():
        kk[0:C, :] = jnp.zeros((C, 128), F32)
        vv[0:C, :] = jnp.zeros((C, 128), F32)

    kk[C:2 * C, :] = k_ref[...]
    vv[C:2 * C, :] = v_ref[...]

    def rows(start, d):
        return pl.ds(start, BLK) if d == 1 else pl.ds(start, BLK, stride=d)

    def run_blocks(bi, d, starts, pens):
        g = len(starts)
        qs, kos, kps, vos, vps = [], [], [], [], []
        for st in starts:
            qs.append(q_ref[rows(st, d), :])
            kos.append(kk[rows(C + st, d), :].astype(BF16))
            kps.append(kk[rows(C + st - BLK * d, d), :].astype(BF16))
            vos.append(vv[rows(C + st, d), :].astype(BF16))
            vps.append(vv[rows(C + st - BLK * d, d), :].astype(BF16))
        s_o, s_p = [], []
        for i in range(g):
            for half in range(2):
                keep = first_head if half == 0 else jnp.logical_not(first_head)
                qm = jnp.where(keep, qs[i], 0.0).astype(BF16)
                s_o.append(lax.dot_general(qm, kos[i], nt, preferred_element_type=F32) + bias_ref[bi, half, 1])
                s_p.append(lax.dot_general(qm, kps[i], nt, preferred_element_type=F32)
                           + (bias_ref[bi, half, 0] + pens[i]))
        ms = [jnp.max(jnp.maximum(a, b), axis=-1, keepdims=True) for a, b in zip(s_o, s_p)]
        p_o = [jnp.exp(a - m) for a, m in zip(s_o, ms)]
        p_p = [jnp.exp(a - m) for a, m in zip(s_p, ms)]
        ls = [jnp.sum(a + b, axis=-1, keepdims=True) for a, b in zip(p_o, p_p)]
        accs = [_dot(p_o[2 * i + h].astype(BF16), vos[i]) + _dot(p_p[2 * i + h].astype(BF16), vps[i])
                for i in range(g) for h in range(2)]
        for i, st in enumerate(starts):
            m_t = jnp.where(first_head, ms[2 * i], ms[2 * i + 1])
            l_t = jnp.where(first_head, ls[2 * i], ls[2 * i + 1])
            a_t = jnp.where(first_head, accs[2 * i], accs[2 * i + 1])
            r = rows(st, d)
            if bi == 0:
                m_s[r, :] = m_t
                l_s[r, :] = l_t
                acc_s[r, :] = a_t
            else:
                m_old = m_s[r, :]
                m_new = jnp.maximum(m_old, m_t)
                ea = jnp.exp(m_old - m_new)
                eb = jnp.exp(m_t - m_new)
                m_s[r, :] = m_new
                l_s[r, :] = l_s[r, :] * ea + l_t * eb
                acc_s[r, :] = acc_s[r, :] * ea + a_t * eb

    n_blocks = C // BLK
    for bi, (_, d) in enumerate(DILATED):
        per_res = C // (BLK * d)

        def group(gi, carry, bi=bi, d=d, per_res=per_res):
            starts, pens = [], []
            for u in range(ATT_GROUP):
                j = gi * ATT_GROUP + u
                n = j % per_res
                starts.append(n * (BLK * d) + j // per_res)
                pens.append(jnp.where((c == 0) & (n == 0), NEG, 0.0).astype(F32))
            run_blocks(bi, d, starts, pens)
            return carry

        lax.fori_loop(0, n_blocks // ATT_GROUP, group, 0)

    o_ref[...] = (acc_s[...] / l_s[...]).astype(o_ref.dtype)
    kk[0:C, :] = kk[C:2 * C, :]
    vv[0:C, :] = vv[C:2 * C, :]


def _attention(slabs, bias, batch, seq):
    C = ATT_CHUNK
    nch = seq // C

    def slab(off):
        return pl.BlockSpec((None, C, 128), lambda b, p, c: (off + p, b * nch + c, 0))

    return pl.pallas_call(
        _attn_body,
        grid=(batch, N_PAIRS, nch),
        in_specs=[slab(0), slab(N_PAIRS), slab(2 * N_PAIRS),
                  pl.BlockSpec((len(DILATED), None, 2, 2, BLK, BLK), lambda b, p, c: (0, p, 0, 0, 0, 0))],
        out_specs=pl.BlockSpec((None, C, 128), lambda b, p, c: (p, b * nch + c, 0)),
        out_shape=jax.ShapeDtypeStruct((N_PAIRS, batch * seq, 128), BF16),
        scratch_shapes=[pltpu.VMEM((2 * C, 128), F32), pltpu.VMEM((2 * C, 128), F32),
                        pltpu.VMEM((C, 128), F32), pltpu.VMEM((C, 128), F32), pltpu.VMEM((C, 128), F32)],
        compiler_params=_params("parallel", "parallel", "arbitrary"),
        name="attn",
    )(slabs, slabs, slabs, bias)


def _attn_bias(slopes):
    i = jnp.arange(BLK)[:, None]
    j = jnp.arange(BLK)[None, :]
    per_branch = []
    for _, dil in DILATED:
        halves = []
        for steps in (i + BLK - j, i - j):
            ok = (steps >= 0) & (steps <= BLK)
            b = -slopes[:, None, None] * (dil * steps).astype(F32)[None]
            halves.append(jnp.where(ok[None], b, NEG))
        per_branch.append(jnp.stack(halves, axis=1))
    return jnp.stack(per_branch).reshape(len(DILATED), N_PAIRS, 2, 2, BLK, BLK)


def _softplus(x):
    return jnp.maximum(x, 0.0) + jnp.log1p(jnp.exp(-jnp.abs(x)))


def _gelu(x):
    return 0.5 * x * (1.0 + jnp.tanh(0.7978845608028654 * (x + 0.044715 * (x * x * x))))


def _lru_terms(xc, wr_ref, br_ref, wi_ref, bi_ref, lam_ref):
    xb = xc.astype(BF16)
    r = jax.nn.sigmoid(_dot(xb, wr_ref[...]) + br_ref[...])
    i = jax.nn.sigmoid(_dot(xb, wi_ref[...]) + bi_ref[...])
    log_a = (-LRU_C * _softplus(-lam_ref[...])) * r
    a = jnp.exp(log_a)
    b = jnp.sqrt(-jnp.tanh(log_a) * (a * a + 1.0)) * (i * xc)
    return a, b


def _scan_body(xr_ref, yg_ref, cw_ref, cb_ref, wr_ref, br_ref, wi_ref, bi_ref, lam_ref,
               rec_ref, hl_ref, xbuf, a_s, b_s, h_s):
    tc = xr_ref.shape[0]

    @pl.when(pl.program_id(1) == 0)
    def _():
        xbuf[0:8, :] = jnp.zeros((8, xbuf.shape[1]), F32)
        h_s[...] = jnp.zeros_like(h_s)

    xbuf[8:, :] = xr_ref[...]
    xc = cb_ref[...]
    for j in range(CONV_WIDTH):
        xc = xc + xbuf[pl.ds(8 - (CONV_WIDTH - 1) + j, tc), :] * cw_ref[j:j + 1, :]
    xbuf[0:8, :] = xbuf[tc:tc + 8, :]

    a, b = _lru_terms(xc, wr_ref, br_ref, wi_ref, bi_ref, lam_ref)
    a_s[...] = a
    b_s[...] = b
    row = lax.broadcasted_iota(jnp.int32, (8, a.shape[1]), 0)

    def group(g, h):
        r0 = pl.multiple_of(g * 8, 8)
        ca = a_s[pl.ds(r0, 8), :]
        cb = b_s[pl.ds(r0, 8), :]
        for s in (1, 2, 4):
            keep = row >= s
            cb = jnp.where(keep, ca * pltpu.roll(cb, s, 0) + cb, cb)
            ca = jnp.where(keep, ca * pltpu.roll(ca, s, 0), ca)
        h8 = ca * h + cb
        b_s[pl.ds(r0, 8), :] = h8
        return h8[7:8, :]

    h = lax.fori_loop(0, tc // 8, group, h_s[...], unroll=8)
    h_s[...] = h
    hl_ref[...] = h
    rec_ref[...] = _gelu(yg_ref[...]) * b_s[...]


def _scan(p32, xr_col, lw, layer, batch, seq):
    tc = SCAN_CHUNK
    nc = seq // tc
    w = ATT_WIDTH
    view = p32.reshape(batch, seq, p32.shape[-1])

    def vec(name):
        return _const_spec((None, 1, w), lambda b, c: (layer, 0, 0)), lw[name]

    specs, args = [], []
    for spec, arr in (
        (pl.BlockSpec((None, tc, w), lambda b, c: (b, c, xr_col)), view),
        (pl.BlockSpec((None, tc, w), lambda b, c: (b, c, xr_col + 1)), view),
        (_const_spec((None, CONV_WIDTH, w), lambda b, c: (layer, 0, 0)), lw["conv_w"]),
        vec("conv_b"),
        (_const_spec((None, w, w), lambda b, c: (layer, 0, 0)), lw["w_rgate"]),
        vec("b_rgate"),
        (_const_spec((None, w, w), lambda b, c: (layer, 0, 0)), lw["w_igate"]),
        vec("b_igate"),
        vec("lru_lambda"),
    ):
        specs.append(spec)
        args.append(arr)
    rec, h_last = pl.pallas_call(
        _scan_body,
        grid=(batch, nc),
        in_specs=specs,
        out_specs=[
            pl.BlockSpec((None, tc, w), lambda b, c: (b, c, 0)),
            pl.BlockSpec((None, 1, w), lambda b, c: (b, 0, 0)),
        ],
        out_shape=[
            jax.ShapeDtypeStruct((batch, seq, w), F32),
            jax.ShapeDtypeStruct((batch, 1, w), F32),
        ],
        scratch_shapes=[
            pltpu.VMEM((tc + 8, w), F32),
            pltpu.VMEM((tc, w), F32),
            pltpu.VMEM((tc, w), F32),
            pltpu.VMEM((1, w), F32),
        ],
        compiler_params=_params("parallel", "arbitrary"),
        name="conv_rglru_scan",
    )(*args)
    return rec.reshape(batch * seq, w), h_last.reshape(batch, w)


def _step_body(xr_ref, yg_ref, sc_ref, h0_ref, cw_ref, cb_ref, wr_ref, br_ref, wi_ref, bi_ref, lam_ref,
               rec_ref, h_ref, conv_ref):
    w = xr_ref.shape[1]
    xr = xr_ref[...]
    xc = cb_ref[...] + xr * cw_ref[CONV_WIDTH - 1:CONV_WIDTH, :]
    for j in range(CONV_WIDTH - 1):
        xc = xc + sc_ref[:, j * w:(j + 1) * w] * cw_ref[j:j + 1, :]
    a, b = _lru_terms(xc, wr_ref, br_ref, wi_ref, bi_ref, lam_ref)
    h = a * h0_ref[...] + b
    h_ref[...] = h
    rec_ref[...] = _gelu(yg_ref[...]) * h
    conv_ref[:, 0:(CONV_WIDTH - 2) * w] = sc_ref[:, w:(CONV_WIDTH - 1) * w]
    conv_ref[:, (CONV_WIDTH - 2) * w:] = xr


def _step(p32, xr_col, state_conv, state_h, lw, layer):
    m = p32.shape[0]
    w = ATT_WIDTH
    sc = state_conv.reshape(state_conv.shape[0], m, (CONV_WIDTH - 1) * w)

    def vec(name):
        return _const_spec((None, 1, w), lambda i: (layer, 0, 0)), lw[name]

    specs, args = [], []
    for spec, arr in (
        (pl.BlockSpec((m, w), lambda i: (0, xr_col)), p32),
        (pl.BlockSpec((m, w), lambda i: (0, xr_col + 1)), p32),
        (pl.BlockSpec((None, m, (CONV_WIDTH - 1) * w), lambda i: (layer, 0, 0)), sc),
        (pl.BlockSpec((None, m, w), lambda i: (layer, 0, 0)), state_h),
        (_const_spec((None, CONV_WIDTH, w), lambda i: (layer, 0, 0)), lw["conv_w"]),
        vec("conv_b"),
        (_const_spec((None, w, w), lambda i: (layer, 0, 0)), lw["w_rgate"]),
        vec("b_rgate"),
        (_const_spec((None, w, w), lambda i: (layer, 0, 0)), lw["w_igate"]),
        vec("b_igate"),
        vec("lru_lambda"),
    ):
        specs.append(spec)
        args.append(arr)
    return pl.pallas_call(
        _step_body,
        grid=(1,),
        in_specs=specs,
        out_specs=[
            pl.BlockSpec((m, w), lambda i: (0, 0)),
            pl.BlockSpec((m, w), lambda i: (0, 0)),
            pl.BlockSpec((m, (CONV_WIDTH - 1) * w), lambda i: (0, 0)),
        ],
        out_shape=[
            jax.ShapeDtypeStruct((m, w), F32),
            jax.ShapeDtypeStruct((m, w), F32),
            jax.ShapeDtypeStruct((m, (CONV_WIDTH - 1) * w), F32),
        ],
        compiler_params=_params("arbitrary"),
        name="conv_rglru_step",
    )(*args)


def _sample_attn_body(qt_ref, knt_ref, vnt_ref, k_ref, v_ref, bias_ref, cnt_ref, o_ref):
    b = pl.program_id(0)
    nh, hd, _ = k_ref.shape
    lane = lax.broadcasted_iota(jnp.int32, qt_ref.shape, 1)
    mine = lane == b

    def column(ref):
        return jnp.sum(jnp.where(mine, ref[...], 0.0), axis=-1, keepdims=True).reshape(nh, hd, 1)

    q = column(qt_ref) * Q_SCALE
    k_new = column(knt_ref)
    v_new = column(vnt_ref)
    s = jnp.sum(k_ref[...] * q, axis=1, keepdims=True) + bias_ref[...]
    s_new = jnp.sum(k_new * q, axis=1, keepdims=True)
    m = jnp.maximum(jnp.max(s, axis=-1, keepdims=True), s_new)
    p = jnp.exp(s - m) * cnt_ref[...]
    p_new = jnp.exp(s_new - m) * float(len(DILATED))
    l = jnp.sum(p, axis=-1, keepdims=True) + p_new
    acc = jnp.sum(p * v_ref[...], axis=-1, keepdims=True) + p_new * v_new
    o = (acc / l).reshape(nh * hd, 1)

    @pl.when(b == 0)
    def _():
        o_ref[...] = jnp.zeros_like(o_ref)

    o_ref[...] = jnp.where(mine, o, o_ref[...])


def _sample_attn(p32, cache_kt, cache_vt, bias, cnt, layer):
    depth, batch, nh, hd, w_buf = cache_kt.shape
    w = nh * hd
    win = bias.shape[-1]
    qkv_t = p32[:, :3 * w].T
    col = lambda c: pl.BlockSpec((w, batch), lambda i: (c, 0))
    cache = pl.BlockSpec((None, None, nh, hd, win), lambda i: (layer, i, 0, 0, w_buf // win - 1))
    out_t = pl.pallas_call(
        _sample_attn_body,
        grid=(batch,),
        in_specs=[col(0), col(1), col(2), cache, cache,
                  _const_spec((nh, 1, win), lambda i: (0, 0, 0)),
                  _const_spec((1, 1, win), lambda i: (0, 0, 0))],
        out_specs=pl.BlockSpec((w, batch), lambda i: (0, 0)),
        out_shape=jax.ShapeDtypeStruct((w, batch), F32),
        compiler_params=_params("arbitrary"),
        name="sample_attn",
    )(qkv_t, qkv_t, qkv_t, cache_kt, cache_vt, bias, cnt)
    return out_t.T


def _sample_tables(slopes):
    dist = MAX_WINDOW - jnp.arange(MAX_WINDOW, dtype=jnp.int32)
    cnt = sum(((dist % dil == 0) & (dist <= win)).astype(F32) for win, dil in DILATED)
    bias = jnp.where(cnt > 0, -slopes[:, None] * dist.astype(F32)[None, :], NEG)
    return bias[:, None, :], cnt[None, None, :]


def _mix_body(att_ref, rec_ref, x_ref, ga_ref, gl_ref, w_ref, gp_ref, out_ref):
    if len(att_ref.shape) == 3:
        att = jnp.concatenate([att_ref[p] for p in range(att_ref.shape[0])], axis=-1).astype(F32)
    else:
        att = att_ref[...]
    half = att.shape[1]
    att_n = _rms(att, ga_ref[...]).astype(BF16)
    rec_n = _rms(rec_ref[...], gl_ref[...]).astype(BF16)
    mixed = _dot(att_n, w_ref[0:half, :]) + _dot(rec_n, w_ref[half:, :])
    out_ref[...] = x_ref[...] + _rms(mixed, gp_ref[...])


def _mix(att, rec, x, lw, layer, tm):
    m, d = x.shape
    w = ATT_WIDTH
    row = lambda width: pl.BlockSpec((tm, width), lambda i: (i, 0))
    att_spec = row(w) if att.ndim == 2 else pl.BlockSpec((att.shape[0], tm, 128), lambda i: (0, i, 0))
    return pl.pallas_call(
        _mix_body,
        grid=(m // tm,),
        in_specs=[
            att_spec, row(w), row(d),
            _const_spec((None, 1, w), lambda i: (layer, 0, 0)),
            _const_spec((None, 1, w), lambda i: (layer, 0, 0)),
            _const_spec((None, 2 * w, d), lambda i: (layer, 0, 0)),
            _const_spec((None, 1, d), lambda i: (layer, 0, 0)),
        ],
        out_specs=row(d),
        out_shape=jax.ShapeDtypeStruct((m, d), F32),
        compiler_params=_params("parallel"),
        name="mix_out",
    )(att, rec, x, lw["g_att_out"], lw["g_lru_out"], lw["w_out"], lw["g_post_mix"])


def _ffn_body(x_ref, g1_ref, wg_ref, wu_ref, wd_ref, g2_ref, out_ref):
    x = x_ref[...]
    h = _rms(x, g1_ref[...]).astype(BF16)
    acc = None
    for f0, f1 in FFN_CHUNKS:
        gate = _dot(h, wg_ref[:, f0:f1])
        up = _dot(h, wu_ref[:, f0:f1])
        act = (gate * jax.nn.sigmoid(gate) * up).astype(BF16)
        part = _dot(act, wd_ref[f0:f1, :])
        acc = part if acc is None else acc + part
    out_ref[...] = x + _rms(acc, g2_ref[...])


def _ffn(x, lw, layer, tm):
    m, d = x.shape
    f = lw["w_ffn_gate"].shape[-1]
    assert FFN_CHUNKS[-1][1] == f
    return pl.pallas_call(
        _ffn_body,
        grid=(m // tm,),
        in_specs=[
            pl.BlockSpec((tm, d), lambda i: (i, 0)),
            _const_spec((None, 1, d), lambda i: (layer, 0, 0)),
            _const_spec((None, d, f), lambda i: (layer, 0, 0)),
            _const_spec((None, d, f), lambda i: (layer, 0, 0)),
            _const_spec((None, f, d), lambda i: (layer, 0, 0)),
            _const_spec((None, 1, d), lambda i: (layer, 0, 0)),
        ],
        out_specs=pl.BlockSpec((tm, d), lambda i: (i, 0)),
        out_shape=jax.ShapeDtypeStruct((m, d), F32),
        compiler_params=_params("parallel"),
        name="ffn",
    )(x, lw["g_pre_ffn"], lw["w_ffn_gate"], lw["w_ffn_up"], lw["w_ffn_down"], lw["g_post_ffn"])


def _block_diag(w):
    depth, nblk, c, _ = w.shape
    eye = jnp.eye(nblk, dtype=w.dtype)
    return jnp.einsum("lncd,nm->lncmd", w, eye).reshape(depth, nblk * c, nblk * c)


def _heads_from_slabs(slabs, batch, seq, n_keep):
    t = slabs.reshape(N_PAIRS, batch, seq, 128)[:, :, seq - n_keep:, :]
    return t.transpose(1, 2, 0, 3).reshape(batch, n_keep, N_HEADS, HEAD_DIM)


def kernel(x_prompt, x_sample, cache_k, cache_v, state_conv, state_h, g_pre_mix, g_post_mix, w_in, w_out, conv_w, conv_b, w_rgate, b_rgate, w_igate, b_igate, lru_lambda, g_att_out, g_lru_out, g_pre_ffn, g_post_ffn, w_ffn_gate, w_ffn_up, w_ffn_down):
    batch, seq, d_model = x_prompt.shape
    dec_batch, dec_seq, _ = x_sample.shape
    depth = w_in.shape[0]
    w = ATT_WIDTH
    assert dec_seq == 1 and seq % ATT_CHUNK == 0 and cache_k.shape[2] % MAX_WINDOW == 0
    n_keep = min(MAX_WINDOW, seq)

    vec = lambda a: a.reshape(depth, 1, a.shape[-1])
    lw = dict(
        g_pre_mix=vec(g_pre_mix), g_post_mix=vec(g_post_mix), g_att_out=vec(g_att_out), g_lru_out=vec(g_lru_out),
        g_pre_ffn=vec(g_pre_ffn), g_post_ffn=vec(g_post_ffn), conv_b=vec(conv_b), b_rgate=vec(b_rgate),
        b_igate=vec(b_igate), lru_lambda=vec(lru_lambda), conv_w=conv_w,
        w_in=w_in.astype(BF16), w_out=w_out.astype(BF16),
        w_rgate=_block_diag(w_rgate).astype(BF16), w_igate=_block_diag(w_igate).astype(BF16),
        w_ffn_gate=w_ffn_gate.astype(BF16), w_ffn_up=w_ffn_up.astype(BF16), w_ffn_down=w_ffn_down.astype(BF16),
    )

    slopes = 2.0 ** (-(8.0 / N_HEADS) * jnp.arange(1, N_HEADS + 1, dtype=F32))
    prompt_bias = _attn_bias(slopes)
    sample_bias, sample_cnt = _sample_tables(slopes)
    cache_kt = cache_k.transpose(0, 1, 3, 4, 2)
    cache_vt = cache_v.transpose(0, 1, 3, 4, 2)

    tm_p = 512
    xp = x_prompt.reshape(batch * seq, d_model)
    xs = x_sample.reshape(dec_batch, d_model)
    kp, vp, cp, hp, ks, vs, cs, hs = [], [], [], [], [], [], [], []
    for l in range(depth):
        slabs, rg = _proj(xp, lw["g_pre_mix"], lw["w_in"], l, tm_p, True)
        att = _attention(slabs, prompt_bias, batch, seq)
        rec, h_last = _scan(rg, 0, lw, l, batch, seq)
        xp = _mix(att, rec, xp, lw, l, tm_p)
        xp = _ffn(xp, lw, l, tm_p)
        kp.append(_heads_from_slabs(slabs[N_PAIRS:2 * N_PAIRS], batch, seq, n_keep))
        vp.append(_heads_from_slabs(slabs[2 * N_PAIRS:], batch, seq, n_keep))
        cp.append(rg.reshape(batch, seq, 2 * w)[:, seq - (CONV_WIDTH - 1):, :w])
        hp.append(h_last)
        (s32,) = _proj(xs, lw["g_pre_mix"], lw["w_in"], l, dec_batch, False)
        att_s = _sample_attn(s32, cache_kt, cache_vt, sample_bias, sample_cnt, l)
        rec_s, h_s, conv_s = _step(s32, 3, state_conv, state_h, lw, l)
        xs = _mix(att_s, rec_s, xs, lw, l, dec_batch)
        xs = _ffn(xs, lw, l, dec_batch)
        ks.append(s32[:, w:2 * w].reshape(dec_batch, 1, N_HEADS, HEAD_DIM))
        vs.append(s32[:, 2 * w:3 * w].reshape(dec_batch, 1, N_HEADS, HEAD_DIM))
        cs.append(conv_s.reshape(dec_batch, CONV_WIDTH - 1, w))
        hs.append(h_s)
    return (xp.reshape(batch, seq, d_model), xs.reshape(dec_batch, 1, d_model),
            jnp.stack(kp), jnp.stack(vp), jnp.stack(cp), jnp.stack(hp),
            jnp.stack(ks), jnp.stack(vs), jnp.stack(cs), jnp.stack(hs))
```

```python
import functools

import jax
import jax.numpy as jnp
from jax import lax
from jax.experimental import pallas as pl
from jax.experimental.pallas import tpu as pltpu

F32 = jnp.float32
BF16 = jnp.bfloat16

N_HEADS = 8
HEAD_DIM = 64
ATT_WIDTH = N_HEADS * HEAD_DIM
N_PAIRS = N_HEADS // 2
CONV_WIDTH = 4
LRU_C = 8.0
DILATED = ((128, 1), (512, 4), (2048, 16))
MAX_WINDOW = 2048
BLK = 128
EPS = 1e-6
NEG = -1e30
Q_SCALE = HEAD_DIM ** -0.5
LOG2E = 1.4426950408889634

VMEM_LIMIT_BYTES = 56 * 1024 * 1024
FFN_CHUNKS = ((0, 1536), (1536, 2816))
SCAN_CHUNK = 512
ATT_CHUNK = BLK * DILATED[-1][1]
ATT_GROUP = 4


def _params(*sem):
    return pltpu.CompilerParams(dimension_semantics=sem, vmem_limit_bytes=VMEM_LIMIT_BYTES)


def _const_spec(shape, index):
    return pl.BlockSpec(shape, index, pipeline_mode=pl.Buffered(1))


def _rms(x, g):
    return x * lax.rsqrt(jnp.mean(x * x, axis=-1, keepdims=True) + EPS) * g


def _dot(a, b):
    return jnp.dot(a, b, preferred_element_type=F32)


def _proj_body(x_ref, g_ref, w_ref, *rest, slabs, sample):
    if sample:
        sample_refs, out_refs = rest[:7], rest[7:-1]
        _sample_rows(sample, *sample_refs, rest[-1])
    else:
        out_refs = rest
    h = _rms(x_ref[...], g_ref[...]).astype(BF16)
    n = w_ref.shape[1]
    n_att = 3 * ATT_WIDTH
    for c in range(n // ATT_WIDTH):
        lo, hi = c * ATT_WIDTH, (c + 1) * ATT_WIDTH
        pc = _dot(h, w_ref[:, lo:hi])
        if not slabs:
            out_refs[0][:, lo:hi] = pc
        elif lo >= n_att:
            out_refs[1][:, lo - n_att:hi - n_att] = pc
        else:
            for p in range(N_PAIRS):
                part = pc[:, p * 128:(p + 1) * 128]
                out_refs[0][c * N_PAIRS + p] = part * (Q_SCALE * LOG2E) if c == 0 else part


def _proj(x, g, w, layer, tm, slabs, sample=None):
    m, d = x.shape
    n = w.shape[-1]
    if slabs:
        n_rest = n - 3 * ATT_WIDTH
        out_specs = [pl.BlockSpec((3 * N_PAIRS, tm, 128), lambda i: (0, i, 0)),
                     pl.BlockSpec((tm, n_rest), lambda i: (i, 0))]
        out_shape = [jax.ShapeDtypeStruct((3 * N_PAIRS, m, 128), F32),
                     jax.ShapeDtypeStruct((m, n_rest), F32)]
    else:
        out_specs = [pl.BlockSpec((tm, n), lambda i: (i, 0))]
        out_shape = [jax.ShapeDtypeStruct((m, n), F32)]
    in_specs = [
        pl.BlockSpec((tm, d), lambda i: (i, 0)),
        _const_spec((None, 1, d), lambda i: (layer, 0, 0)),
        _const_spec((None, d, n), lambda i: (layer, 0, 0)),
    ]
    args = [x, g, w]
    if sample:
        sa_specs, sa_args, sa_out_spec, sa_out_shape = _sample_operands(sample, m // tm)
        in_specs += sa_specs
        args += sa_args
        out_specs.append(sa_out_spec)
        out_shape.append(sa_out_shape)
    return pl.pallas_call(
        functools.partial(_proj_body, slabs=slabs, sample=sample[0] if sample else None),
        grid=(m // tm,),
        in_specs=in_specs,
        out_specs=out_specs,
        out_shape=out_shape,
        compiler_params=_params("arbitrary" if sample else "parallel"),
        name="proj_in",
    )(*args)


def _attn_body(q_ref, k_ref, v_ref, bias_ref, o_ref, kk, vv, t_a, t_b, m_a, m_b):
    t_s = (t_a, t_b)
    m_s = (m_a, m_b)
    c = pl.program_id(2)
    C = q_ref.shape[0]
    nt = (((1,), (1,)), ((), ()))
    first_head = lax.broadcasted_iota(jnp.int32, (1, 128), 1) < HEAD_DIM
    heads = (first_head, jnp.logical_not(first_head))

    @pl.when(c == 0)
    def _():
        kk[0:C, :] = jnp.zeros((C, 128), F32)
        vv[0:C, :] = jnp.zeros((C, 128), F32)

    kk[C:2 * C, :] = k_ref[...]
    vv[C:2 * C, :] = v_ref[...]

    def rows(start, n, d):
        return pl.ds(start, n) if d == 1 else pl.ds(start, n, stride=d)

    def run_blocks(bi, d, starts, firsts, init):
        g = len(starts)
        qs, ks, vs = [], [], []
        for st in starts:
            qs.append(q_ref[rows(st, BLK, d), :])
            ks.append(kk[rows(C + st - BLK * d, 2 * BLK, d), :].astype(BF16))
            vs.append(vv[rows(C + st - BLK * d, 2 * BLK, d), :].astype(BF16))
        ss = []
        for i in range(g):
            for h in range(2):
                qm = jnp.where(heads[h], qs[i], 0.0).astype(BF16)
                ss.append(lax.dot_general(qm, ks[i], nt, preferred_element_type=F32) + bias_ref[bi, h, firsts[i]])
        ms = [jnp.max(a, axis=-1, keepdims=True) for a in ss]
        ps = [jnp.exp2(a - m).astype(BF16) for a, m in zip(ss, ms)]
        ts = [_dot(ps[2 * i + h], jnp.where(heads[h], vs[i], jnp.ones_like(vs[i])))
              for i in range(g) for h in range(2)]
        for i, st in enumerate(starts):
            r = rows(st, BLK, d)
            for h in range(2):
                m_t = jnp.broadcast_to(ms[2 * i + h], (BLK, 128))
                t_t = ts[2 * i + h]
                if init:
                    m_s[h][r, :] = m_t
                    t_s[h][r, :] = t_t
                else:
                    m_old = m_s[h][r, :]
                    m_new = jnp.maximum(m_old, m_t)
                    m_s[h][r, :] = m_new
                    t_s[h][r, :] = t_s[h][r, :] * jnp.exp2(m_old - m_new) + t_t * jnp.exp2(m_t - m_new)

    n_blocks = C // BLK
    order = sorted(range(len(DILATED)), key=lambda b: -DILATED[b][1])
    for pos, bi in enumerate(order):
        d = DILATED[bi][1]
        per_res = C // (BLK * d)

        def group(gi, carry, bi=bi, d=d, per_res=per_res, init=(pos == 0)):
            starts, firsts = [], []
            for u in range(ATT_GROUP):
                j = gi * ATT_GROUP + u
                n = j % per_res
                starts.append(n * (BLK * d) + j // per_res)
                firsts.append(((c == 0) & (n == 0)).astype(jnp.int32))
            run_blocks(bi, d, starts, firsts, init)
            return carry

        lax.fori_loop(0, n_blocks // ATT_GROUP, group, 0)

    ta = t_a[...]
    tb = t_b[...]
    o_ref[...] = jnp.where(first_head, ta / pltpu.roll(ta, HEAD_DIM, 1),
                           tb / pltpu.roll(tb, HEAD_DIM, 1)).astype(o_ref.dtype)
    kk[0:C, :] = kk[C:2 * C, :]
    vv[0:C, :] = vv[C:2 * C, :]


def _attention(slabs, bias, batch, seq):
    C = ATT_CHUNK
    nch = seq // C

    def slab(off):
        return pl.BlockSpec((None, C, 128), lambda b, p, c: (off + p, b * nch + c, 0))

    state = pltpu.VMEM((C, 128), F32)
    return pl.pallas_call(
        _attn_body,
        grid=(batch, N_PAIRS, nch),
        in_specs=[slab(0), slab(N_PAIRS), slab(2 * N_PAIRS),
                  pl.BlockSpec((len(DILATED), None, 2, 2, BLK, 2 * BLK), lambda b, p, c: (0, p, 0, 0, 0, 0))],
        out_specs=pl.BlockSpec((None, C, 128), lambda b, p, c: (p, b * nch + c, 0)),
        out_shape=jax.ShapeDtypeStruct((N_PAIRS, batch * seq, 128), BF16),
        scratch_shapes=[pltpu.VMEM((2 * C, 128), F32), pltpu.VMEM((2 * C, 128), F32), state, state, state, state],
        compiler_params=_params("parallel", "parallel", "arbitrary"),
        name="attn",
    )(slabs, slabs, slabs, bias)


def _attn_bias(slopes):
    i = jnp.arange(BLK)[:, None]
    j = jnp.arange(2 * BLK)[None, :]
    steps = i + BLK - j
    ok = (steps >= 0) & (steps <= BLK)
    out = []
    for _, dil in DILATED:
        b = (-LOG2E * slopes)[:, None, None] * (dil * steps).astype(F32)[None]
        out.append(jnp.stack([jnp.where(m[None], b, NEG) for m in (ok, ok & (j >= BLK))], axis=1))
    return jnp.stack(out).reshape(len(DILATED), N_PAIRS, 2, 2, BLK, 2 * BLK)


def _softplus(x):
    return jnp.maximum(x, 0.0) + jnp.log1p(jnp.exp(-jnp.abs(x)))


def _gelu(x):
    return 0.5 * x * (1.0 + jnp.tanh(0.7978845608028654 * (x + 0.044715 * (x * x * x))))


def _lru_terms(xc, wr_ref, br_ref, wi_ref, bi_ref, lam_ref):
    xb = xc.astype(BF16)
    r = jax.nn.sigmoid(_dot(xb, wr_ref[...]) + br_ref[...])
    i = jax.nn.sigmoid(_dot(xb, wi_ref[...]) + bi_ref[...])
    log_a = (-LRU_C * _softplus(-lam_ref[...])) * r
    a = jnp.exp(log_a)
    b = jnp.sqrt(-jnp.tanh(log_a) * (a * a + 1.0)) * (i * xc)
    return a, b


def _scan_body(xr_ref, yg_ref, cw_ref, cb_ref, wr_ref, br_ref, wi_ref, bi_ref, lam_ref,
               rec_ref, hl_ref, xbuf, a_s, b_s, h_s):
    tc = xr_ref.shape[0]

    @pl.when(pl.program_id(1) == 0)
    def _():
        xbuf[0:8, :] = jnp.zeros((8, xbuf.shape[1]), F32)
        h_s[...] = jnp.zeros_like(h_s)

    xbuf[8:, :] = xr_ref[...]
    xc = cb_ref[...]
    for j in range(CONV_WIDTH):
        xc = xc + xbuf[pl.ds(8 - (CONV_WIDTH - 1) + j, tc), :] * cw_ref[j:j + 1, :]
    xbuf[0:8, :] = xbuf[tc:tc + 8, :]

    a, b = _lru_terms(xc, wr_ref, br_ref, wi_ref, bi_ref, lam_ref)
    a_s[...] = a
    b_s[...] = b
    row = lax.broadcasted_iota(jnp.int32, (8, a.shape[1]), 0)

    def group(g, h):
        r0 = pl.multiple_of(g * 8, 8)
        ca = a_s[pl.ds(r0, 8), :]
        cb = b_s[pl.ds(r0, 8), :]
        for s in (1, 2, 4):
            keep = row >= s
            cb = jnp.where(keep, ca * pltpu.roll(cb, s, 0) + cb, cb)
            ca = jnp.where(keep, ca * pltpu.roll(ca, s, 0), ca)
        h8 = ca * h + cb
        b_s[pl.ds(r0, 8), :] = h8
        return h8[7:8, :]

    h = lax.fori_loop(0, tc // 8, group, h_s[...], unroll=8)
    h_s[...] = h
    hl_ref[...] = h
    rec_ref[...] = _gelu(yg_ref[...]) * b_s[...]


def _scan(p32, xr_col, lw, layer, batch, seq):
    tc = SCAN_CHUNK
    nc = seq // tc
    w = ATT_WIDTH
    view = p32.reshape(batch, seq, p32.shape[-1])

    def vec(name):
        return _const_spec((None, 1, w), lambda b, c: (layer, 0, 0)), lw[name]

    specs, args = [], []
    for spec, arr in (
        (pl.BlockSpec((None, tc, w), lambda b, c: (b, c, xr_col)), view),
        (pl.BlockSpec((None, tc, w), lambda b, c: (b, c, xr_col + 1)), view),
        (_const_spec((None, CONV_WIDTH, w), lambda b, c: (layer, 0, 0)), lw["conv_w"]),
        vec("conv_b"),
        (_const_spec((None, w, w), lambda b, c: (layer, 0, 0)), lw["w_rgate"]),
        vec("b_rgate"),
        (_const_spec((None, w, w), lambda b, c: (layer, 0, 0)), lw["w_igate"]),
        vec("b_igate"),
        vec("lru_lambda"),
    ):
        specs.append(spec)
        args.append(arr)
    rec, h_last = pl.pallas_call(
        _scan_body,
        grid=(batch, nc),
        in_specs=specs,
        out_specs=[
            pl.BlockSpec((None, tc, w), lambda b, c: (b, c, 0)),
            pl.BlockSpec((None, 1, w), lambda b, c: (b, 0, 0)),
        ],
        out_shape=[
            jax.ShapeDtypeStruct((batch, seq, w), F32),
            jax.ShapeDtypeStruct((batch, 1, w), F32),
        ],
        scratch_shapes=[
            pltpu.VMEM((tc + 8, w), F32),
            pltpu.VMEM((tc, w), F32),
            pltpu.VMEM((tc, w), F32),
            pltpu.VMEM((1, w), F32),
        ],
        compiler_params=_params("parallel", "arbitrary"),
        name="conv_rglru_scan",
    )(*args)
    return rec.reshape(batch * seq, w), h_last.reshape(batch, w)


def _step_body(xr_ref, yg_ref, sc_ref, h0_ref, cw_ref, cb_ref, wr_ref, br_ref, wi_ref, bi_ref, lam_ref,
               rec_ref, h_ref, conv_ref):
    w = xr_ref.shape[1]
    xr = xr_ref[...]
    xc = cb_ref[...] + xr * cw_ref[CONV_WIDTH - 1:CONV_WIDTH, :]
    for j in range(CONV_WIDTH - 1):
        xc = xc + sc_ref[:, j * w:(j + 1) * w] * cw_ref[j:j + 1, :]
    a, b = _lru_terms(xc, wr_ref, br_ref, wi_ref, bi_ref, lam_ref)
    h = a * h0_ref[...] + b
    h_ref[...] = h
    rec_ref[...] = _gelu(yg_ref[...]) * h
    conv_ref[:, 0:(CONV_WIDTH - 2) * w] = sc_ref[:, w:(CONV_WIDTH - 1) * w]
    conv_ref[:, (CONV_WIDTH - 2) * w:] = xr


def _step(p32, xr_col, state_conv, state_h, lw, layer):
    m = p32.shape[0]
    w = ATT_WIDTH
    sc = state_conv.reshape(state_conv.shape[0], m, (CONV_WIDTH - 1) * w)

    def vec(name):
        return _const_spec((None, 1, w), lambda i: (layer, 0, 0)), lw[name]

    specs, args = [], []
    for spec, arr in (
        (pl.BlockSpec((m, w), lambda i: (0, xr_col)), p32),
        (pl.BlockSpec((m, w), lambda i: (0, xr_col + 1)), p32),
        (pl.BlockSpec((None, m, (CONV_WIDTH - 1) * w), lambda i: (layer, 0, 0)), sc),
        (pl.BlockSpec((None, m, w), lambda i: (layer, 0, 0)), state_h),
        (_const_spec((None, CONV_WIDTH, w), lambda i: (layer, 0, 0)), lw["conv_w"]),
        vec("conv_b"),
        (_const_spec((None, w, w), lambda i: (layer, 0, 0)), lw["w_rgate"]),
        vec("b_rgate"),
        (_const_spec((None, w, w), lambda i: (layer, 0, 0)), lw["w_igate"]),
        vec("b_igate"),
        vec("lru_lambda"),
    ):
        specs.append(spec)
        args.append(arr)
    return pl.pallas_call(
        _step_body,
        grid=(1,),
        in_specs=specs,
        out_specs=[
            pl.BlockSpec((m, w), lambda i: (0, 0)),
            pl.BlockSpec((m, w), lambda i: (0, 0)),
            pl.BlockSpec((m, (CONV_WIDTH - 1) * w), lambda i: (0, 0)),
        ],
        out_shape=[
            jax.ShapeDtypeStruct((m, w), F32),
            jax.ShapeDtypeStruct((m, w), F32),
            jax.ShapeDtypeStruct((m, (CONV_WIDTH - 1) * w), F32),
        ],
        compiler_params=_params("arbitrary"),
        name="conv_rglru_step",
    )(*args)


def _sample_rows(cfg, qt_ref, knt_ref, vnt_ref, k_ref, v_ref, bias_ref, cnt_ref, o_ref):
    first_row, n_rows = cfg
    step = pl.program_id(0)
    _, nh, hd, _ = k_ref.shape
    lane = lax.broadcasted_iota(jnp.int32, qt_ref.shape, 1)

    @pl.when(step == 0)
    def _():
        o_ref[...] = jnp.zeros_like(o_ref)

    for j in range(n_rows):
        mine = lane == first_row + step * n_rows + j

        def column(ref):
            return jnp.sum(jnp.where(mine, ref[...], 0.0), axis=-1, keepdims=True).reshape(nh, hd, 1)

        q = column(qt_ref) * Q_SCALE
        k_new = column(knt_ref)
        v_new = column(vnt_ref)
        s = jnp.sum(k_ref[j] * q, axis=1, keepdims=True) + bias_ref[...]
        s_new = jnp.sum(k_new * q, axis=1, keepdims=True)
        m = jnp.maximum(jnp.max(s, axis=-1, keepdims=True), s_new)
        p = jnp.exp(s - m) * cnt_ref[...]
        p_new = jnp.exp(s_new - m) * float(len(DILATED))
        l = jnp.sum(p, axis=-1, keepdims=True) + p_new
        acc = jnp.sum(p * v_ref[j], axis=-1, keepdims=True) + p_new * v_new
        o_ref[...] = jnp.where(mine, (acc / l).reshape(nh * hd, 1), o_ref[...])


def _sample_operands(sample_args, n_steps):
    (first_row, n_rows), qkv_t, cache_kt, cache_vt, bias, cnt, layer = sample_args
    _, batch, nh, hd, w_buf = cache_kt.shape
    w = nh * hd
    win = bias.shape[-1]
    assert first_row % n_rows == 0 and first_row + n_rows * n_steps <= batch
    col = lambda c: _const_spec((w, batch), lambda i: (c, 0))
    cache = pl.BlockSpec((None, n_rows, nh, hd, win),
                         lambda i: (layer, first_row // n_rows + i, 0, 0, w_buf // win - 1))
    specs = [col(0), col(1), col(2), cache, cache,
             _const_spec((nh, 1, win), lambda i: (0, 0, 0)), _const_spec((1, 1, win), lambda i: (0, 0, 0))]
    args = [qkv_t, qkv_t, qkv_t, cache_kt, cache_vt, bias, cnt]
    return specs, args, pl.BlockSpec((w, batch), lambda i: (0, 0)), jax.ShapeDtypeStruct((w, batch), F32)


def _sample_tables(slopes):
    dist = MAX_WINDOW - jnp.arange(MAX_WINDOW, dtype=jnp.int32)
    cnt = sum(((dist % dil == 0) & (dist <= win)).astype(F32) for win, dil in DILATED)
    bias = jnp.where(cnt > 0, -slopes[:, None] * dist.astype(F32)[None, :], NEG)
    return bias[:, None, :], cnt[None, None, :]


def _mix_body(att_ref, rec_ref, x_ref, ga_ref, gl_ref, w_ref, gp_ref, *rest, sample):
    if sample:
        _sample_rows(sample, *rest[:7], rest[-1])
    out_ref = rest[-2] if sample else rest[-1]
    if len(att_ref.shape) == 3:
        att = jnp.concatenate([att_ref[p] for p in range(att_ref.shape[0])], axis=-1).astype(F32)
    else:
        att = att_ref[...]
    half = att.shape[1]
    att_n = _rms(att, ga_ref[...]).astype(BF16)
    rec_n = _rms(rec_ref[...], gl_ref[...]).astype(BF16)
    mixed = _dot(att_n, w_ref[0:half, :]) + _dot(rec_n, w_ref[half:, :])
    out_ref[...] = x_ref[...] + _rms(mixed, gp_ref[...])


def _mix(att, rec, x, lw, layer, tm, sample=None):
    m, d = x.shape
    w = ATT_WIDTH
    row = lambda width: pl.BlockSpec((tm, width), lambda i: (i, 0))
    att_spec = row(w) if att.ndim == 2 else pl.BlockSpec((att.shape[0], tm, 128), lambda i: (0, i, 0))
    in_specs = [
        att_spec, row(w), row(d),
        _const_spec((None, 1, w), lambda i: (layer, 0, 0)),
        _const_spec((None, 1, w), lambda i: (layer, 0, 0)),
        _const_spec((None, 2 * w, d), lambda i: (layer, 0, 0)),
        _const_spec((None, 1, d), lambda i: (layer, 0, 0)),
    ]
    args = [att, rec, x, lw["g_att_out"], lw["g_lru_out"], lw["w_out"], lw["g_post_mix"]]
    out_specs = [row(d)]
    out_shape = [jax.ShapeDtypeStruct((m, d), F32)]
    if sample:
        sa_specs, sa_args, sa_out_spec, sa_out_shape = _sample_operands(sample, m // tm)
        in_specs += sa_specs
        args += sa_args
        out_specs.append(sa_out_spec)
        out_shape.append(sa_out_shape)
    return pl.pallas_call(
        functools.partial(_mix_body, sample=sample[0] if sample else None),
        grid=(m // tm,),
        in_specs=in_specs,
        out_specs=out_specs,
        out_shape=out_shape,
        compiler_params=_params("arbitrary" if sample else "parallel"),
        name="mix_out",
    )(*args)


def _ffn_body(x_ref, g1_ref, wg_ref, wu_ref, wd_ref, g2_ref, out_ref):
    x = x_ref[...]
    h = _rms(x, g1_ref[...]).astype(BF16)
    acc = None
    for f0, f1 in FFN_CHUNKS:
        gate = _dot(h, wg_ref[:, f0:f1])
        up = _dot(h, wu_ref[:, f0:f1])
        act = (gate * jax.nn.sigmoid(gate) * up).astype(BF16)
        part = _dot(act, wd_ref[f0:f1, :])
        acc = part if acc is None else acc + part
    out_ref[...] = x + _rms(acc, g2_ref[...])


def _ffn(x, lw, layer, tm):
    m, d = x.shape
    f = lw["w_ffn_gate"].shape[-1]
    assert FFN_CHUNKS[-1][1] == f
    return pl.pallas_call(
        _ffn_body,
        grid=(m // tm,),
        in_specs=[
            pl.BlockSpec((tm, d), lambda i: (i, 0)),
            _const_spec((None, 1, d), lambda i: (layer, 0, 0)),
            _const_spec((None, d, f), lambda i: (layer, 0, 0)),
            _const_spec((None, d, f), lambda i: (layer, 0, 0)),
            _const_spec((None, f, d), lambda i: (layer, 0, 0)),
            _const_spec((None, 1, d), lambda i: (layer, 0, 0)),
        ],
        out_specs=pl.BlockSpec((tm, d), lambda i: (i, 0)),
        out_shape=jax.ShapeDtypeStruct((m, d), F32),
        compiler_params=_params("parallel"),
        name="ffn",
    )(x, lw["g_pre_ffn"], lw["w_ffn_gate"], lw["w_ffn_up"], lw["w_ffn_down"], lw["g_post_ffn"])


def _block_diag(w):
    depth, nblk, c, _ = w.shape
    eye = jnp.eye(nblk, dtype=w.dtype)
    return jnp.einsum("lncd,nm->lncmd", w, eye).reshape(depth, nblk * c, nblk * c)


def _heads_from_slabs(slabs, batch, seq, n_keep):
    t = slabs.reshape(N_PAIRS, batch, seq, 128)[:, :, seq - n_keep:, :]
    return t.transpose(1, 2, 0, 3).reshape(batch, n_keep, N_HEADS, HEAD_DIM)


def kernel(x_prompt, x_sample, cache_k, cache_v, state_conv, state_h, g_pre_mix, g_post_mix, w_in, w_out, conv_w, conv_b, w_rgate, b_rgate, w_igate, b_igate, lru_lambda, g_att_out, g_lru_out, g_pre_ffn, g_post_ffn, w_ffn_gate, w_ffn_up, w_ffn_down):
    batch, seq, d_model = x_prompt.shape
    dec_batch, dec_seq, _ = x_sample.shape
    depth = w_in.shape[0]
    w = ATT_WIDTH
    assert dec_seq == 1 and seq % ATT_CHUNK == 0 and cache_k.shape[2] % MAX_WINDOW == 0
    n_keep = min(MAX_WINDOW, seq)

    vec = lambda a: a.reshape(depth, 1, a.shape[-1])
    lw = dict(
        g_pre_mix=vec(g_pre_mix), g_post_mix=vec(g_post_mix), g_att_out=vec(g_att_out), g_lru_out=vec(g_lru_out),
        g_pre_ffn=vec(g_pre_ffn), g_post_ffn=vec(g_post_ffn), conv_b=vec(conv_b), b_rgate=vec(b_rgate),
        b_igate=vec(b_igate), lru_lambda=vec(lru_lambda), conv_w=conv_w,
        w_in=w_in.astype(BF16), w_out=w_out.astype(BF16),
        w_rgate=_block_diag(w_rgate).astype(BF16), w_igate=_block_diag(w_igate).astype(BF16),
        w_ffn_gate=w_ffn_gate.astype(BF16), w_ffn_up=w_ffn_up.astype(BF16), w_ffn_down=w_ffn_down.astype(BF16),
    )

    slopes = 2.0 ** (-(8.0 / N_HEADS) * jnp.arange(1, N_HEADS + 1, dtype=F32))
    prompt_bias = _attn_bias(slopes)
    sample_bias, sample_cnt = _sample_tables(slopes)
    cache_kt = cache_k.transpose(0, 1, 3, 4, 2)
    cache_vt = cache_v.transpose(0, 1, 3, 4, 2)

    tm_p = 512
    tm_proj = 256
    rows_proj = (0, 1)
    rows_mix = ((batch * seq) // tm_proj, 2)
    assert rows_mix[0] + rows_mix[1] * ((batch * seq) // tm_p) == dec_batch
    xp = x_prompt.reshape(batch * seq, d_model)
    xs = x_sample.reshape(dec_batch, d_model)
    kp, vp, cp, hp, ks, vs, cs, hs = [], [], [], [], [], [], [], []
    for l in range(depth):
        (s32,) = _proj(xs, lw["g_pre_mix"], lw["w_in"], l, dec_batch, False)
        sample = (s32[:, :3 * w].T, cache_kt, cache_vt, sample_bias, sample_cnt, l)
        slabs, rg, att_s0 = _proj(xp, lw["g_pre_mix"], lw["w_in"], l, tm_proj, True, (rows_proj,) + sample)
        att = _attention(slabs, prompt_bias, batch, seq)
        rec, h_last = _scan(rg, 0, lw, l, batch, seq)
        xp, att_s1 = _mix(att, rec, xp, lw, l, tm_p, (rows_mix,) + sample)
        xp = _ffn(xp, lw, l, tm_p)
        kp.append(_heads_from_slabs(slabs[N_PAIRS:2 * N_PAIRS], batch, seq, n_keep))
        vp.append(_heads_from_slabs(slabs[2 * N_PAIRS:], batch, seq, n_keep))
        cp.append(rg.reshape(batch, seq, 2 * w)[:, seq - (CONV_WIDTH - 1):, :w])
        hp.append(h_last)
        att_s = (att_s0 + att_s1).T
        rec_s, h_s, conv_s = _step(s32, 3, state_conv, state_h, lw, l)
        (xs,) = _mix(att_s, rec_s, xs, lw, l, dec_batch)
        xs = _ffn(xs, lw, l, dec_batch)
        ks.append(s32[:, w:2 * w].reshape(dec_batch, 1, N_HEADS, HEAD_DIM))
        vs.append(s32[:, 2 * w:3 * w].reshape(dec_batch, 1, N_HEADS, HEAD_DIM))
        cs.append(conv_s.reshape(dec_batch, CONV_WIDTH - 1, w))
        hs.append(h_s)
    return (xp.reshape(batch, seq, d_model), xs.reshape(dec_batch, 1, d_model),
            jnp.stack(kp), jnp.stack(vp), jnp.stack(cp), jnp.stack(hp),
            jnp.stack(ks), jnp.stack(vs), jnp.stack(cs), jnp.stack(hs))
```

```python
import functools

import jax
import jax.numpy as jnp
from jax import lax
from jax.experimental import pallas as pl
from jax.experimental.pallas import tpu as pltpu

F32 = jnp.float32
BF16 = jnp.bfloat16

N_HEADS = 8
HEAD_DIM = 64
ATT_WIDTH = N_HEADS * HEAD_DIM
N_PAIRS = N_HEADS // 2
CONV_WIDTH = 4
LRU_C = 8.0
DILATED = ((128, 1), (512, 4), (2048, 16))
MAX_WINDOW = 2048
BLK = 128
EPS = 1e-6
NEG = -1e30
Q_SCALE = HEAD_DIM ** -0.5
LOG2E = 1.4426950408889634

VMEM_LIMIT_BYTES = 56 * 1024 * 1024
FFN_CHUNKS = ((0, 1536), (1536, 2816))
SCAN_CHUNK = 512
ATT_CHUNK = BLK * DILATED[-1][1]
ATT_GROUP = 4


def _params(*sem):
    return pltpu.CompilerParams(dimension_semantics=sem, vmem_limit_bytes=VMEM_LIMIT_BYTES)


def _const_spec(shape, index):
    return pl.BlockSpec(shape, index, pipeline_mode=pl.Buffered(1))


def _rms(x, g):
    return x * lax.rsqrt(jnp.mean(x * x, axis=-1, keepdims=True) + EPS) * g


def _dot(a, b):
    return jnp.dot(a, b, preferred_element_type=F32)


def _proj_body(x_ref, g_ref, w_ref, *rest, slabs, sample):
    if sample:
        sample_refs, out_refs = rest[:7], rest[7:-1]
        _sample_rows(sample, pl.program_id(0), *sample_refs, rest[-1])
    else:
        out_refs = rest
    h = _rms(x_ref[...], g_ref[...]).astype(BF16)
    n = w_ref.shape[1]
    n_att = 3 * ATT_WIDTH
    for c in range(n // ATT_WIDTH):
        lo, hi = c * ATT_WIDTH, (c + 1) * ATT_WIDTH
        pc = _dot(h, w_ref[:, lo:hi])
        if not slabs:
            out_refs[0][:, lo:hi] = pc
        elif lo >= n_att:
            out_refs[1][:, lo - n_att:hi - n_att] = pc
        else:
            for p in range(N_PAIRS):
                part = pc[:, p * 128:(p + 1) * 128]
                out_refs[0][c * N_PAIRS + p] = part * (Q_SCALE * LOG2E) if c == 0 else part


def _proj(x, g, w, layer, tm, slabs, sample=None):
    m, d = x.shape
    n = w.shape[-1]
    if slabs:
        n_rest = n - 3 * ATT_WIDTH
        out_specs = [pl.BlockSpec((3 * N_PAIRS, tm, 128), lambda i: (0, i, 0)),
                     pl.BlockSpec((tm, n_rest), lambda i: (i, 0))]
        out_shape = [jax.ShapeDtypeStruct((3 * N_PAIRS, m, 128), F32),
                     jax.ShapeDtypeStruct((m, n_rest), F32)]
    else:
        out_specs = [pl.BlockSpec((tm, n), lambda i: (i, 0))]
        out_shape = [jax.ShapeDtypeStruct((m, n), F32)]
    in_specs = [
        pl.BlockSpec((tm, d), lambda i: (i, 0)),
        _const_spec((None, 1, d), lambda i: (layer, 0, 0)),
        _const_spec((None, d, n), lambda i: (layer, 0, 0)),
    ]
    args = [x, g, w]
    if sample:
        sa_specs, sa_args, sa_out_spec, sa_out_shape = _sample_operands(sample, (m // tm,))
        in_specs += sa_specs
        args += sa_args
        out_specs.append(sa_out_spec)
        out_shape.append(sa_out_shape)
    return pl.pallas_call(
        functools.partial(_proj_body, slabs=slabs, sample=sample[0] if sample else None),
        grid=(m // tm,),
        in_specs=in_specs,
        out_specs=out_specs,
        out_shape=out_shape,
        compiler_params=_params("arbitrary" if sample else "parallel"),
        name="proj_in",
    )(*args)


def _attn_body(q_ref, k_ref, v_ref, bias_ref, *rest, sample):
    if sample:
        grid = tuple(pl.num_programs(a) for a in range(3))
        step = _linear_step(grid)(*(pl.program_id(a) for a in range(3)))
        _sample_rows(sample, step, *rest[:7], rest[-7])
        rest = rest[7:-7] + rest[-6:]
    o_ref, kk, vv, t_a, t_b, m_a, m_b = rest
    t_s = (t_a, t_b)
    m_s = (m_a, m_b)
    c = pl.program_id(2)
    C = q_ref.shape[0]
    nt = (((1,), (1,)), ((), ()))
    first_head = lax.broadcasted_iota(jnp.int32, (1, 128), 1) < HEAD_DIM
    heads = (first_head, jnp.logical_not(first_head))

    @pl.when(c == 0)
    def _():
        kk[0:C, :] = jnp.zeros((C, 128), F32)
        vv[0:C, :] = jnp.zeros((C, 128), F32)

    kk[C:2 * C, :] = k_ref[...]
    vv[C:2 * C, :] = v_ref[...]

    def rows(start, n, d):
        return pl.ds(start, n) if d == 1 else pl.ds(start, n, stride=d)

    def run_blocks(bi, d, starts, firsts, init):
        g = len(starts)
        qs, ks, vs = [], [], []
        for st in starts:
            qs.append(q_ref[rows(st, BLK, d), :])
            ks.append(kk[rows(C + st - BLK * d, 2 * BLK, d), :].astype(BF16))
            vs.append(vv[rows(C + st - BLK * d, 2 * BLK, d), :].astype(BF16))
        ss = []
        for i in range(g):
            for h in range(2):
                qm = jnp.where(heads[h], qs[i], 0.0).astype(BF16)
                ss.append(lax.dot_general(qm, ks[i], nt, preferred_element_type=F32) + bias_ref[bi, h, firsts[i]])
        ms = [jnp.max(a, axis=-1, keepdims=True) for a in ss]
        ps = [jnp.exp2(a - m).astype(BF16) for a, m in zip(ss, ms)]
        ts = [_dot(ps[2 * i + h], jnp.where(heads[h], vs[i], jnp.ones_like(vs[i])))
              for i in range(g) for h in range(2)]
        for i, st in enumerate(starts):
            r = rows(st, BLK, d)
            for h in range(2):
                m_t = jnp.broadcast_to(ms[2 * i + h], (BLK, 128))
                t_t = ts[2 * i + h]
                if init:
                    m_s[h][r, :] = m_t
                    t_s[h][r, :] = t_t
                else:
                    m_old = m_s[h][r, :]
                    m_new = jnp.maximum(m_old, m_t)
                    m_s[h][r, :] = m_new
                    t_s[h][r, :] = t_s[h][r, :] * jnp.exp2(m_old - m_new) + t_t * jnp.exp2(m_t - m_new)

    n_blocks = C // BLK
    order = sorted(range(len(DILATED)), key=lambda b: -DILATED[b][1])
    for pos, bi in enumerate(order):
        d = DILATED[bi][1]
        per_res = C // (BLK * d)

        def group(gi, carry, bi=bi, d=d, per_res=per_res, init=(pos == 0)):
            starts, firsts = [], []
            for u in range(ATT_GROUP):
                j = gi * ATT_GROUP + u
                n = j % per_res
                starts.append(n * (BLK * d) + j // per_res)
                firsts.append(((c == 0) & (n == 0)).astype(jnp.int32))
            run_blocks(bi, d, starts, firsts, init)
            return carry

        lax.fori_loop(0, n_blocks // ATT_GROUP, group, 0)

    ta = t_a[...]
    tb = t_b[...]
    o_ref[...] = jnp.where(first_head, ta / pltpu.roll(ta, HEAD_DIM, 1),
                           tb / pltpu.roll(tb, HEAD_DIM, 1)).astype(o_ref.dtype)
    kk[0:C, :] = kk[C:2 * C, :]
    vv[0:C, :] = vv[C:2 * C, :]


def _attention(slabs, bias, batch, seq, sample=None):
    C = ATT_CHUNK
    nch = seq // C
    grid = (batch, N_PAIRS, nch)

    def slab(off):
        return pl.BlockSpec((None, C, 128), lambda b, p, c: (off + p, b * nch + c, 0))

    state = pltpu.VMEM((C, 128), F32)
    in_specs = [slab(0), slab(N_PAIRS), slab(2 * N_PAIRS),
                pl.BlockSpec((len(DILATED), None, 2, 2, BLK, 2 * BLK), lambda b, p, c: (0, p, 0, 0, 0, 0))]
    args = [slabs, slabs, slabs, bias]
    out_specs = [pl.BlockSpec((None, C, 128), lambda b, p, c: (p, b * nch + c, 0))]
    out_shape = [jax.ShapeDtypeStruct((N_PAIRS, batch * seq, 128), BF16)]
    if sample:
        sa_specs, sa_args, sa_out_spec, sa_out_shape = _sample_operands(sample, grid)
        in_specs += sa_specs
        args += sa_args
        out_specs.append(sa_out_spec)
        out_shape.append(sa_out_shape)
    return pl.pallas_call(
        functools.partial(_attn_body, sample=sample[0] if sample else None),
        grid=grid,
        in_specs=in_specs,
        out_specs=out_specs,
        out_shape=out_shape,
        scratch_shapes=[pltpu.VMEM((2 * C, 128), F32), pltpu.VMEM((2 * C, 128), F32), state, state, state, state],
        compiler_params=_params(*(("arbitrary",) * 3 if sample else ("parallel", "parallel", "arbitrary"))),
        name="attn",
    )(*args)


def _attn_bias(slopes):
    i = jnp.arange(BLK)[:, None]
    j = jnp.arange(2 * BLK)[None, :]
    steps = i + BLK - j
    ok = (steps >= 0) & (steps <= BLK)
    out = []
    for _, dil in DILATED:
        b = (-LOG2E * slopes)[:, None, None] * (dil * steps).astype(F32)[None]
        out.append(jnp.stack([jnp.where(m[None], b, NEG) for m in (ok, ok & (j >= BLK))], axis=1))
    return jnp.stack(out).reshape(len(DILATED), N_PAIRS, 2, 2, BLK, 2 * BLK)


def _softplus(x):
    return jnp.maximum(x, 0.0) + jnp.log1p(jnp.exp(-jnp.abs(x)))


def _gelu(x):
    return 0.5 * x * (1.0 + jnp.tanh(0.7978845608028654 * (x + 0.044715 * (x * x * x))))


def _lru_terms(xc, wr_ref, br_ref, wi_ref, bi_ref, lam_ref):
    xb = xc.astype(BF16)
    r = jax.nn.sigmoid(_dot(xb, wr_ref[...]) + br_ref[...])
    i = jax.nn.sigmoid(_dot(xb, wi_ref[...]) + bi_ref[...])
    log_a = (-LRU_C * _softplus(-lam_ref[...])) * r
    a = jnp.exp(log_a)
    b = jnp.sqrt(-jnp.tanh(log_a) * (a * a + 1.0)) * (i * xc)
    return a, b


def _scan_body(xr_ref, yg_ref, cw_ref, cb_ref, wr_ref, br_ref, wi_ref, bi_ref, lam_ref,
               rec_ref, hl_ref, xbuf, a_s, b_s, h_s):
    tc = xr_ref.shape[0]

    @pl.when(pl.program_id(1) == 0)
    def _():
        xbuf[0:8, :] = jnp.zeros((8, xbuf.shape[1]), F32)
        h_s[...] = jnp.zeros_like(h_s)

    xbuf[8:, :] = xr_ref[...]
    xc = cb_ref[...]
    for j in range(CONV_WIDTH):
        xc = xc + xbuf[pl.ds(8 - (CONV_WIDTH - 1) + j, tc), :] * cw_ref[j:j + 1, :]
    xbuf[0:8, :] = xbuf[tc:tc + 8, :]

    a, b = _lru_terms(xc, wr_ref, br_ref, wi_ref, bi_ref, lam_ref)
    a_s[...] = a
    b_s[...] = b
    row = lax.broadcasted_iota(jnp.int32, (8, a.shape[1]), 0)

    def group(g, h):
        r0 = pl.multiple_of(g * 8, 8)
        ca = a_s[pl.ds(r0, 8), :]
        cb = b_s[pl.ds(r0, 8), :]
        for s in (1, 2, 4):
            keep = row >= s
            cb = jnp.where(keep, ca * pltpu.roll(cb, s, 0) + cb, cb)
            ca = jnp.where(keep, ca * pltpu.roll(ca, s, 0), ca)
        h8 = ca * h + cb
        b_s[pl.ds(r0, 8), :] = h8
        return h8[7:8, :]

    h = lax.fori_loop(0, tc // 8, group, h_s[...], unroll=8)
    h_s[...] = h
    hl_ref[...] = h
    rec_ref[...] = _gelu(yg_ref[...]) * b_s[...]


def _scan(p32, xr_col, lw, layer, batch, seq):
    tc = SCAN_CHUNK
    nc = seq // tc
    w = ATT_WIDTH
    view = p32.reshape(batch, seq, p32.shape[-1])

    def vec(name):
        return _const_spec((None, 1, w), lambda b, c: (layer, 0, 0)), lw[name]

    specs, args = [], []
    for spec, arr in (
        (pl.BlockSpec((None, tc, w), lambda b, c: (b, c, xr_col)), view),
        (pl.BlockSpec((None, tc, w), lambda b, c: (b, c, xr_col + 1)), view),
        (_const_spec((None, CONV_WIDTH, w), lambda b, c: (layer, 0, 0)), lw["conv_w"]),
        vec("conv_b"),
        (_const_spec((None, w, w), lambda b, c: (layer, 0, 0)), lw["w_rgate"]),
        vec("b_rgate"),
        (_const_spec((None, w, w), lambda b, c: (layer, 0, 0)), lw["w_igate"]),
        vec("b_igate"),
        vec("lru_lambda"),
    ):
        specs.append(spec)
        args.append(arr)
    rec, h_last = pl.pallas_call(
        _scan_body,
        grid=(batch, nc),
        in_specs=specs,
        out_specs=[
            pl.BlockSpec((None, tc, w), lambda b, c: (b, c, 0)),
            pl.BlockSpec((None, 1, w), lambda b, c: (b, 0, 0)),
        ],
        out_shape=[
            jax.ShapeDtypeStruct((batch, seq, w), F32),
            jax.ShapeDtypeStruct((batch, 1, w), F32),
        ],
        scratch_shapes=[
            pltpu.VMEM((tc + 8, w), F32),
            pltpu.VMEM((tc, w), F32),
            pltpu.VMEM((tc, w), F32),
            pltpu.VMEM((1, w), F32),
        ],
        compiler_params=_params("parallel", "arbitrary"),
        name="conv_rglru_scan",
    )(*args)
    return rec.reshape(batch * seq, w), h_last.reshape(batch, w)


def _step_body(xr_ref, yg_ref, sc_ref, h0_ref, cw_ref, cb_ref, wr_ref, br_ref, wi_ref, bi_ref, lam_ref,
               rec_ref, h_ref, conv_ref):
    w = xr_ref.shape[1]
    xr = xr_ref[...]
    xc = cb_ref[...] + xr * cw_ref[CONV_WIDTH - 1:CONV_WIDTH, :]
    for j in range(CONV_WIDTH - 1):
        xc = xc + sc_ref[:, j * w:(j + 1) * w] * cw_ref[j:j + 1, :]
    a, b = _lru_terms(xc, wr_ref, br_ref, wi_ref, bi_ref, lam_ref)
    h = a * h0_ref[...] + b
    h_ref[...] = h
    rec_ref[...] = _gelu(yg_ref[...]) * h
    conv_ref[:, 0:(CONV_WIDTH - 2) * w] = sc_ref[:, w:(CONV_WIDTH - 1) * w]
    conv_ref[:, (CONV_WIDTH - 2) * w:] = xr


def _step(p32, xr_col, state_conv, state_h, lw, layer):
    m = p32.shape[0]
    w = ATT_WIDTH
    sc = state_conv.reshape(state_conv.shape[0], m, (CONV_WIDTH - 1) * w)

    def vec(name):
        return _const_spec((None, 1, w), lambda i: (layer, 0, 0)), lw[name]

    specs, args = [], []
    for spec, arr in (
        (pl.BlockSpec((m, w), lambda i: (0, xr_col)), p32),
        (pl.BlockSpec((m, w), lambda i: (0, xr_col + 1)), p32),
        (pl.BlockSpec((None, m, (CONV_WIDTH - 1) * w), lambda i: (layer, 0, 0)), sc),
        (pl.BlockSpec((None, m, w), lambda i: (layer, 0, 0)), state_h),
        (_const_spec((None, CONV_WIDTH, w), lambda i: (layer, 0, 0)), lw["conv_w"]),
        vec("conv_b"),
        (_const_spec((None, w, w), lambda i: (layer, 0, 0)), lw["w_rgate"]),
        vec("b_rgate"),
        (_const_spec((None, w, w), lambda i: (layer, 0, 0)), lw["w_igate"]),
        vec("b_igate"),
        vec("lru_lambda"),
    ):
        specs.append(spec)
        args.append(arr)
    return pl.pallas_call(
        _step_body,
        grid=(1,),
        in_specs=specs,
        out_specs=[
            pl.BlockSpec((m, w), lambda i: (0, 0)),
            pl.BlockSpec((m, w), lambda i: (0, 0)),
            pl.BlockSpec((m, (CONV_WIDTH - 1) * w), lambda i: (0, 0)),
        ],
        out_shape=[
            jax.ShapeDtypeStruct((m, w), F32),
            jax.ShapeDtypeStruct((m, w), F32),
            jax.ShapeDtypeStruct((m, (CONV_WIDTH - 1) * w), F32),
        ],
        compiler_params=_params("arbitrary"),
        name="conv_rglru_step",
    )(*args)


def _sample_rows(cfg, step, qt_ref, knt_ref, vnt_ref, k_ref, v_ref, bias_ref, cnt_ref, o_ref):
    first_row, n_rows = cfg
    _, nh, hd, _ = k_ref.shape
    lane = lax.broadcasted_iota(jnp.int32, qt_ref.shape, 1)

    @pl.when(step == 0)
    def _():
        o_ref[...] = jnp.zeros_like(o_ref)

    for j in range(n_rows):
        mine = lane == first_row + step * n_rows + j

        def column(ref):
            return jnp.sum(jnp.where(mine, ref[...], 0.0), axis=-1, keepdims=True).reshape(nh, hd, 1)

        q = column(qt_ref) * Q_SCALE
        k_new = column(knt_ref)
        v_new = column(vnt_ref)
        s = jnp.sum(k_ref[j] * q, axis=1, keepdims=True) + bias_ref[...]
        s_new = jnp.sum(k_new * q, axis=1, keepdims=True)
        m = jnp.maximum(jnp.max(s, axis=-1, keepdims=True), s_new)
        p = jnp.exp(s - m) * cnt_ref[...]
        p_new = jnp.exp(s_new - m) * float(len(DILATED))
        l = jnp.sum(p, axis=-1, keepdims=True) + p_new
        acc = jnp.sum(p * v_ref[j], axis=-1, keepdims=True) + p_new * v_new
        o_ref[...] = jnp.where(mine, (acc / l).reshape(nh * hd, 1), o_ref[...])


def _linear_step(grid):
    def lin(*idx):
        out = 0
        for i, n in zip(idx, grid):
            out = out * n + i
        return out
    return lin


def _sample_operands(sample_args, grid):
    (first_row, n_rows), qkv_t, cache_kt, cache_vt, bias, cnt, layer = sample_args
    _, batch, nh, hd, w_buf = cache_kt.shape
    w = nh * hd
    win = bias.shape[-1]
    n_steps = 1
    for n in grid:
        n_steps *= n
    assert first_row % n_rows == 0 and first_row + n_rows * n_steps <= batch
    lin = _linear_step(grid)
    col = lambda c: _const_spec((w, batch), lambda *i: (c, 0))
    cache = pl.BlockSpec((None, n_rows, nh, hd, win),
                         lambda *i: (layer, first_row // n_rows + lin(*i), 0, 0, w_buf // win - 1))
    specs = [col(0), col(1), col(2), cache, cache,
             _const_spec((nh, 1, win), lambda *i: (0, 0, 0)), _const_spec((1, 1, win), lambda *i: (0, 0, 0))]
    args = [qkv_t, qkv_t, qkv_t, cache_kt, cache_vt, bias, cnt]
    return specs, args, pl.BlockSpec((w, batch), lambda *i: (0, 0)), jax.ShapeDtypeStruct((w, batch), F32)


def _sample_tables(slopes):
    dist = MAX_WINDOW - jnp.arange(MAX_WINDOW, dtype=jnp.int32)
    cnt = sum(((dist % dil == 0) & (dist <= win)).astype(F32) for win, dil in DILATED)
    bias = jnp.where(cnt > 0, -slopes[:, None] * dist.astype(F32)[None, :], NEG)
    return bias[:, None, :], cnt[None, None, :]


def _mix_body(att_ref, rec_ref, x_ref, ga_ref, gl_ref, w_ref, gp_ref, *rest, sample):
    if sample:
        _sample_rows(sample, pl.program_id(0), *rest[:7], rest[-1])
    out_ref = rest[-2] if sample else rest[-1]
    if len(att_ref.shape) == 3:
        att = jnp.concatenate([att_ref[p] for p in range(att_ref.shape[0])], axis=-1).astype(F32)
    else:
        att = att_ref[...]
    half = att.shape[1]
    att_n = _rms(att, ga_ref[...]).astype(BF16)
    rec_n = _rms(rec_ref[...], gl_ref[...]).astype(BF16)
    mixed = _dot(att_n, w_ref[0:half, :]) + _dot(rec_n, w_ref[half:, :])
    out_ref[...] = x_ref[...] + _rms(mixed, gp_ref[...])


def _mix(att, rec, x, lw, layer, tm, sample=None):
    m, d = x.shape
    w = ATT_WIDTH
    row = lambda width: pl.BlockSpec((tm, width), lambda i: (i, 0))
    att_spec = row(w) if att.ndim == 2 else pl.BlockSpec((att.shape[0], tm, 128), lambda i: (0, i, 0))
    in_specs = [
        att_spec, row(w), row(d),
        _const_spec((None, 1, w), lambda i: (layer, 0, 0)),
        _const_spec((None, 1, w), lambda i: (layer, 0, 0)),
        _const_spec((None, 2 * w, d), lambda i: (layer, 0, 0)),
        _const_spec((None, 1, d), lambda i: (layer, 0, 0)),
    ]
    args = [att, rec, x, lw["g_att_out"], lw["g_lru_out"], lw["w_out"], lw["g_post_mix"]]
    out_specs = [row(d)]
    out_shape = [jax.ShapeDtypeStruct((m, d), F32)]
    if sample:
        sa_specs, sa_args, sa_out_spec, sa_out_shape = _sample_operands(sample, (m // tm,))
        in_specs += sa_specs
        args += sa_args
        out_specs.append(sa_out_spec)
        out_shape.append(sa_out_shape)
    return pl.pallas_call(
        functools.partial(_mix_body, sample=sample[0] if sample else None),
        grid=(m // tm,),
        in_specs=in_specs,
        out_specs=out_specs,
        out_shape=out_shape,
        compiler_params=_params("arbitrary" if sample else "parallel"),
        name="mix_out",
    )(*args)


def _ffn_body(x_ref, g1_ref, wg_ref, wu_ref, wd_ref, g2_ref, *rest, sample):
    if sample:
        _sample_rows(sample, pl.program_id(0), *rest[:7], rest[-1])
    out_ref = rest[-2] if sample else rest[-1]
    x = x_ref[...]
    h = _rms(x, g1_ref[...]).astype(BF16)
    acc = None
    for f0, f1 in FFN_CHUNKS:
        gate = _dot(h, wg_ref[:, f0:f1])
        up = _dot(h, wu_ref[:, f0:f1])
        act = (gate * jax.nn.sigmoid(gate) * up).astype(BF16)
        part = _dot(act, wd_ref[f0:f1, :])
        acc = part if acc is None else acc + part
    out_ref[...] = x + _rms(acc, g2_ref[...])


def _ffn(x, lw, layer, tm, sample=None):
    m, d = x.shape
    f = lw["w_ffn_gate"].shape[-1]
    assert FFN_CHUNKS[-1][1] == f
    in_specs = [
        pl.BlockSpec((tm, d), lambda i: (i, 0)),
        _const_spec((None, 1, d), lambda i: (layer, 0, 0)),
        _const_spec((None, d, f), lambda i: (layer, 0, 0)),
        _const_spec((None, d, f), lambda i: (layer, 0, 0)),
        _const_spec((None, f, d), lambda i: (layer, 0, 0)),
        _const_spec((None, 1, d), lambda i: (layer, 0, 0)),
    ]
    args = [x, lw["g_pre_ffn"], lw["w_ffn_gate"], lw["w_ffn_up"], lw["w_ffn_down"], lw["g_post_ffn"]]
    out_specs = [pl.BlockSpec((tm, d), lambda i: (i, 0))]
    out_shape = [jax.ShapeDtypeStruct((m, d), F32)]
    if sample:
        sa_specs, sa_args, sa_out_spec, sa_out_shape = _sample_operands(sample, (m // tm,))
        in_specs += sa_specs
        args += sa_args
        out_specs.append(sa_out_spec)
        out_shape.append(sa_out_shape)
    return pl.pallas_call(
        functools.partial(_ffn_body, sample=sample[0] if sample else None),
        grid=(m // tm,),
        in_specs=in_specs,
        out_specs=out_specs,
        out_shape=out_shape,
        compiler_params=_params("arbitrary" if sample else "parallel"),
        name="ffn",
    )(*args)


def _block_diag(w):
    depth, nblk, c, _ = w.shape
    eye = jnp.eye(nblk, dtype=w.dtype)
    return jnp.einsum("lncd,nm->lncmd", w, eye).reshape(depth, nblk * c, nblk * c)


def _heads_from_slabs(slabs, first, batch, seq, n_keep):
    t = slabs.reshape(slabs.shape[0], batch, seq, 128)[first:first + N_PAIRS, :, seq - n_keep:, :]
    return t.transpose(1, 2, 0, 3).reshape(batch, n_keep, N_HEADS, HEAD_DIM)


def kernel(x_prompt, x_sample, cache_k, cache_v, state_conv, state_h, g_pre_mix, g_post_mix, w_in, w_out, conv_w, conv_b, w_rgate, b_rgate, w_igate, b_igate, lru_lambda, g_att_out, g_lru_out, g_pre_ffn, g_post_ffn, w_ffn_gate, w_ffn_up, w_ffn_down):
    batch, seq, d_model = x_prompt.shape
    dec_batch, dec_seq, _ = x_sample.shape
    depth = w_in.shape[0]
    w = ATT_WIDTH
    assert dec_seq == 1 and seq % ATT_CHUNK == 0 and cache_k.shape[2] % MAX_WINDOW == 0
    n_keep = min(MAX_WINDOW, seq)

    vec = lambda a: a.reshape(depth, 1, a.shape[-1])
    lw = dict(
        g_pre_mix=vec(g_pre_mix), g_post_mix=vec(g_post_mix), g_att_out=vec(g_att_out), g_lru_out=vec(g_lru_out),
        g_pre_ffn=vec(g_pre_ffn), g_post_ffn=vec(g_post_ffn), conv_b=vec(conv_b), b_rgate=vec(b_rgate),
        b_igate=vec(b_igate), lru_lambda=vec(lru_lambda), conv_w=conv_w,
        w_in=w_in.astype(BF16), w_out=w_out.astype(BF16),
        w_rgate=_block_diag(w_rgate).astype(BF16), w_igate=_block_diag(w_igate).astype(BF16),
        w_ffn_gate=w_ffn_gate.astype(BF16), w_ffn_up=w_ffn_up.astype(BF16), w_ffn_down=w_ffn_down.astype(BF16),
    )

    slopes = 2.0 ** (-(8.0 / N_HEADS) * jnp.arange(1, N_HEADS + 1, dtype=F32))
    prompt_bias = _attn_bias(slopes)
    sample_bias, sample_cnt = _sample_tables(slopes)
    cache_kt = cache_k.transpose(0, 1, 3, 4, 2)
    cache_vt = cache_v.transpose(0, 1, 3, 4, 2)

    tm_p = 512
    n_tiles = (batch * seq) // tm_p
    n_att_steps = batch * N_PAIRS * (seq // ATT_CHUNK)
    hosts = [(0, 1), (n_tiles, 1), (n_tiles + n_att_steps, 1), (2 * n_tiles + n_att_steps, 1)]
    assert 3 * n_tiles + n_att_steps == dec_batch
    xp = x_prompt.reshape(batch * seq, d_model)
    xs = x_sample.reshape(dec_batch, d_model)
    kp, vp, cp, hp, ks, vs, cs, hs = [], [], [], [], [], [], [], []
    for l in range(depth):
        (s32,) = _proj(xs, lw["g_pre_mix"], lw["w_in"], l, dec_batch, False)
        sample = (s32[:, :3 * w].T, cache_kt, cache_vt, sample_bias, sample_cnt, l)
        slabs, rg, att_s0 = _proj(xp, lw["g_pre_mix"], lw["w_in"], l, tm_p, True, (hosts[0],) + sample)
        att, att_s1 = _attention(slabs, prompt_bias, batch, seq, (hosts[1],) + sample)
        rec, h_last = _scan(rg, 0, lw, l, batch, seq)
        xp, att_s2 = _mix(att, rec, xp, lw, l, tm_p, (hosts[2],) + sample)
        xp, att_s3 = _ffn(xp, lw, l, tm_p, (hosts[3],) + sample)
        kp.append(_heads_from_slabs(slabs, N_PAIRS, batch, seq, n_keep))
        vp.append(_heads_from_slabs(slabs, 2 * N_PAIRS, batch, seq, n_keep))
        cp.append(rg.reshape(batch, seq, 2 * w)[:, seq - (CONV_WIDTH - 1):, :w])
        hp.append(h_last)
        att_s = (att_s0 + att_s1 + att_s2 + att_s3).T
        rec_s, h_s, conv_s = _step(s32, 3, state_conv, state_h, lw, l)
        (xs,) = _mix(att_s, rec_s, xs, lw, l, dec_batch)
        (xs,) = _ffn(xs, lw, l, dec_batch)
        ks.append(s32[:, w:2 * w].reshape(dec_batch, 1, N_HEADS, HEAD_DIM))
        vs.append(s32[:, 2 * w:3 * w].reshape(dec_batch, 1, N_HEADS, HEAD_DIM))
        cs.append(conv_s.reshape(dec_batch, CONV_WIDTH - 1, w))
        hs.append(h_s)
    return (xp.reshape(batch, seq, d_model), xs.reshape(dec_batch, 1, d_model),
            jnp.stack(kp), jnp.stack(vp), jnp.stack(cp), jnp.stack(hp),
            jnp.stack(ks), jnp.stack(vs), jnp.stack(cs), jnp.stack(hs))
```

```python
import functools

import jax
import jax.numpy as jnp
from jax import lax
from jax.experimental import pallas as pl
from jax.experimental.pallas import tpu as pltpu

F32 = jnp.float32
BF16 = jnp.bfloat16

N_HEADS = 8
HEAD_DIM = 64
ATT_WIDTH = N_HEADS * HEAD_DIM
N_PAIRS = N_HEADS // 2
CONV_WIDTH = 4
LRU_C = 8.0
DILATED = ((128, 1), (512, 4), (2048, 16))
MAX_WINDOW = 2048
BLK = 128
EPS = 1e-6
NEG = -1e30
Q_SCALE = HEAD_DIM ** -0.5
LOG2E = 1.4426950408889634

VMEM_LIMIT_BYTES = 56 * 1024 * 1024
FFN_CHUNKS = ((0, 1536), (1536, 2816))
SCAN_CHUNK = 512
ATT_CHUNK = BLK * DILATED[-1][1]
ATT_GROUP = 4


def _params(*sem):
    return pltpu.CompilerParams(dimension_semantics=sem, vmem_limit_bytes=VMEM_LIMIT_BYTES)


def _const_spec(shape, index):
    return pl.BlockSpec(shape, index, pipeline_mode=pl.Buffered(1))


def _rms(x, g):
    return x * lax.rsqrt(jnp.mean(x * x, axis=-1, keepdims=True) + EPS) * g


def _dot(a, b):
    return jnp.dot(a, b, preferred_element_type=F32)


def _proj_body(x_ref, g_ref, w_ref, *rest, slabs, sample):
    if sample:
        sample_refs, out_refs = rest[:7], rest[7:-1]
        _sample_rows(sample, pl.program_id(0), *sample_refs, rest[-1])
    else:
        out_refs = rest
    h = _rms(x_ref[...], g_ref[...]).astype(BF16)
    n = w_ref.shape[1]
    n_att = 3 * ATT_WIDTH
    for c in range(n // ATT_WIDTH):
        lo, hi = c * ATT_WIDTH, (c + 1) * ATT_WIDTH
        pc = _dot(h, w_ref[:, lo:hi])
        if not slabs:
            out_refs[0][:, lo:hi] = pc
        elif lo >= n_att:
            out_refs[1][:, lo - n_att:hi - n_att] = pc
        else:
            for p in range(N_PAIRS):
                part = pc[:, p * 128:(p + 1) * 128]
                out_refs[0][c * N_PAIRS + p] = part * (Q_SCALE * LOG2E) if c == 0 else part


def _proj(x, g, w, layer, tm, slabs, sample=None):
    m, d = x.shape
    n = w.shape[-1]
    if slabs:
        n_rest = n - 3 * ATT_WIDTH
        out_specs = [pl.BlockSpec((3 * N_PAIRS, tm, 128), lambda i: (0, i, 0)),
                     pl.BlockSpec((tm, n_rest), lambda i: (i, 0))]
        out_shape = [jax.ShapeDtypeStruct((3 * N_PAIRS, m, 128), F32),
                     jax.ShapeDtypeStruct((m, n_rest), F32)]
    else:
        out_specs = [pl.BlockSpec((tm, n), lambda i: (i, 0))]
        out_shape = [jax.ShapeDtypeStruct((m, n), F32)]
    in_specs = [
        pl.BlockSpec((tm, d), lambda i: (i, 0)),
        _const_spec((None, 1, d), lambda i: (layer, 0, 0)),
        _const_spec((None, d, n), lambda i: (layer, 0, 0)),
    ]
    args = [x, g, w]
    if sample:
        sa_specs, sa_args, sa_out_spec, sa_out_shape = _sample_operands(sample, (m // tm,))
        in_specs += sa_specs
        args += sa_args
        out_specs.append(sa_out_spec)
        out_shape.append(sa_out_shape)
    return pl.pallas_call(
        functools.partial(_proj_body, slabs=slabs, sample=sample[0] if sample else None),
        grid=(m // tm,),
        in_specs=in_specs,
        out_specs=out_specs,
        out_shape=out_shape,
        compiler_params=_params("arbitrary" if sample else "parallel"),
        name="proj_in",
    )(*args)


def _attn_body(q_ref, k_ref, v_ref, kp_ref, vp_ref, bias_ref, *rest, sample):
    if sample:
        grid = tuple(pl.num_programs(a) for a in range(3))
        step = _linear_step(grid)(*(pl.program_id(a) for a in range(3)))
        _sample_rows(sample, step, *rest[:7], rest[-5])
        rest = rest[7:-5] + rest[-4:]
    o_ref, t_a, t_b, m_a, m_b = rest
    t_s = (t_a, t_b)
    m_s = (m_a, m_b)
    c = pl.program_id(2)
    C = q_ref.shape[0]
    nt = (((1,), (1,)), ((), ()))
    first_head = lax.broadcasted_iota(jnp.int32, (1, 128), 1) < HEAD_DIM
    heads = (first_head, jnp.logical_not(first_head))

    def rows(start, d):
        return pl.ds(start, BLK) if d == 1 else pl.ds(start, BLK, stride=d)

    def prev_own(cur_ref, prev_ref, st, d, head_of_chunk):
        own = cur_ref[rows(st, d), :]
        prev = prev_ref[rows(C - BLK * d + st, d), :] if head_of_chunk else cur_ref[rows(st - BLK * d, d), :]
        return jnp.concatenate([prev.astype(BF16), own.astype(BF16)], axis=0)

    def run_blocks(bi, d, blocks, init):
        g = len(blocks)
        qs = [q_ref[rows(st, d), :] for st, _ in blocks]
        ks = [prev_own(k_ref, kp_ref, st, d, hd) for st, hd in blocks]
        vs = [prev_own(v_ref, vp_ref, st, d, hd) for st, hd in blocks]
        variant = [(c == 0).astype(jnp.int32) if hd else 0 for _, hd in blocks]
        ss = []
        for i in range(g):
            for h in range(2):
                qm = jnp.where(heads[h], qs[i], 0.0).astype(BF16)
                ss.append(lax.dot_general(qm, ks[i], nt, preferred_element_type=F32) + bias_ref[bi, h, variant[i]])
        ms = [jnp.max(a, axis=-1, keepdims=True) for a in ss]
        ps = [jnp.exp2(a - m).astype(BF16) for a, m in zip(ss, ms)]
        ts = [_dot(ps[2 * i + h], jnp.where(heads[h], vs[i], jnp.ones_like(vs[i])))
              for i in range(g) for h in range(2)]
        for i, (st, _) in enumerate(blocks):
            r = rows(st, d)
            for h in range(2):
                m_t = jnp.broadcast_to(ms[2 * i + h], (BLK, 128))
                t_t = ts[2 * i + h]
                if init:
                    m_s[h][r, :] = m_t
                    t_s[h][r, :] = t_t
                else:
                    m_old = m_s[h][r, :]
                    m_new = jnp.maximum(m_old, m_t)
                    m_s[h][r, :] = m_new
                    t_s[h][r, :] = t_s[h][r, :] * jnp.exp2(m_old - m_new) + t_t * jnp.exp2(m_t - m_new)

    n_groups = C // (BLK * ATT_GROUP)
    order = sorted(range(len(DILATED)), key=lambda b: -DILATED[b][1])
    for pos, bi in enumerate(order):
        d = DILATED[bi][1]
        per_res = C // (BLK * d)
        init = pos == 0

        def group(gi, bi=bi, d=d, per_res=per_res, init=init):
            blocks = []
            for u in range(ATT_GROUP):
                j = gi * ATT_GROUP + u
                if per_res <= ATT_GROUP:
                    n = u % per_res
                    blocks.append((n * (BLK * d) + j // per_res, n == 0))
                else:
                    blocks.append((j * (BLK * d), isinstance(j, int) and j == 0))
            run_blocks(bi, d, blocks, init)

        if per_res <= ATT_GROUP:
            assert ATT_GROUP % per_res == 0
            lax.fori_loop(0, n_groups, lambda gi, carry: (group(gi), carry)[1], 0)
        else:
            assert per_res == C // BLK
            group(0)
            lax.fori_loop(1, n_groups, lambda gi, carry: (group(gi), carry)[1], 0)

    ta = t_a[...]
    tb = t_b[...]
    o_ref[...] = jnp.where(first_head, ta / pltpu.roll(ta, HEAD_DIM, 1),
                           tb / pltpu.roll(tb, HEAD_DIM, 1)).astype(o_ref.dtype)


def _attention(slabs, bias, batch, seq, sample=None):
    C = ATT_CHUNK
    nch = seq // C
    grid = (batch, N_PAIRS, nch)

    def slab(off, back=0):
        return pl.BlockSpec((None, C, 128), lambda b, p, c: (off + p, b * nch + jnp.maximum(c - back, 0), 0))

    state = pltpu.VMEM((C, 128), F32)
    in_specs = [slab(0), slab(N_PAIRS), slab(2 * N_PAIRS), slab(N_PAIRS, 1), slab(2 * N_PAIRS, 1),
                pl.BlockSpec((len(DILATED), None, 2, 2, BLK, 2 * BLK), lambda b, p, c: (0, p, 0, 0, 0, 0))]
    args = [slabs, slabs, slabs, slabs, slabs, bias]
    out_specs = [pl.BlockSpec((None, C, 128), lambda b, p, c: (p, b * nch + c, 0))]
    out_shape = [jax.ShapeDtypeStruct((N_PAIRS, batch * seq, 128), BF16)]
    if sample:
        sa_specs, sa_args, sa_out_spec, sa_out_shape = _sample_operands(sample, grid)
        in_specs += sa_specs
        args += sa_args
        out_specs.append(sa_out_spec)
        out_shape.append(sa_out_shape)
    return pl.pallas_call(
        functools.partial(_attn_body, sample=sample[0] if sample else None),
        grid=grid,
        in_specs=in_specs,
        out_specs=out_specs,
        out_shape=out_shape,
        scratch_shapes=[state, state, state, state],
        compiler_params=_params(*(("arbitrary",) * 3 if sample else ("parallel", "parallel", "arbitrary"))),
        name="attn",
    )(*args)


def _attn_bias(slopes):
    i = jnp.arange(BLK)[:, None]
    j = jnp.arange(2 * BLK)[None, :]
    steps = i + BLK - j
    ok = (steps >= 0) & (steps <= BLK)
    out = []
    for _, dil in DILATED:
        b = (-LOG2E * slopes)[:, None, None] * (dil * steps).astype(F32)[None]
        out.append(jnp.stack([jnp.where(m[None], b, NEG) for m in (ok, ok & (j >= BLK))], axis=1))
    return jnp.stack(out).reshape(len(DILATED), N_PAIRS, 2, 2, BLK, 2 * BLK)


def _softplus(x):
    return jnp.maximum(x, 0.0) + jnp.log1p(jnp.exp(-jnp.abs(x)))


def _gelu(x):
    return 0.5 * x * (1.0 + jnp.tanh(0.7978845608028654 * (x + 0.044715 * (x * x * x))))


def _sigmoid(x):
    return 0.5 * jnp.tanh(0.5 * x) + 0.5


def _lru_terms(xc, wr_ref, br_ref, wi_ref, bi_ref, lam_ref):
    xb = xc.astype(BF16)
    r = _sigmoid(_dot(xb, wr_ref[...]) + br_ref[...])
    i = _sigmoid(_dot(xb, wi_ref[...]) + bi_ref[...])
    log_a = (-LRU_C * _softplus(-lam_ref[...])) * r
    a = jnp.exp(log_a)
    b = jnp.sqrt(-jnp.tanh(log_a) * (a * a + 1.0)) * (i * xc)
    return a, b


def _scan_body(xr_ref, yg_ref, cw_ref, cb_ref, wr_ref, br_ref, wi_ref, bi_ref, lam_ref,
               rec_ref, hl_ref, xbuf, a_s, b_s, h_s):
    tc = xr_ref.shape[0]

    @pl.when(pl.program_id(1) == 0)
    def _():
        xbuf[0:8, :] = jnp.zeros((8, xbuf.shape[1]), F32)
        h_s[...] = jnp.zeros_like(h_s)

    xbuf[8:, :] = xr_ref[...]
    xc = cb_ref[...]
    for j in range(CONV_WIDTH):
        xc = xc + xbuf[pl.ds(8 - (CONV_WIDTH - 1) + j, tc), :] * cw_ref[j:j + 1, :]
    xbuf[0:8, :] = xbuf[tc:tc + 8, :]

    a, b = _lru_terms(xc, wr_ref, br_ref, wi_ref, bi_ref, lam_ref)
    a_s[...] = a
    b_s[...] = b
    row = lax.broadcasted_iota(jnp.int32, (8, a.shape[1]), 0)

    def group(g, h):
        r0 = pl.multiple_of(g * 8, 8)
        ca = a_s[pl.ds(r0, 8), :]
        cb = b_s[pl.ds(r0, 8), :]
        for s in (1, 2, 4):
            keep = row >= s
            cb = jnp.where(keep, ca * pltpu.roll(cb, s, 0) + cb, cb)
            ca = jnp.where(keep, ca * pltpu.roll(ca, s, 0), ca)
        h8 = ca * h + cb
        b_s[pl.ds(r0, 8), :] = h8
        return h8[7:8, :]

    h = lax.fori_loop(0, tc // 8, group, h_s[...], unroll=8)
    h_s[...] = h
    hl_ref[...] = h
    rec_ref[...] = _gelu(yg_ref[...]) * b_s[...]


def _scan(p32, xr_col, lw, layer, batch, seq):
    tc = SCAN_CHUNK
    nc = seq // tc
    w = ATT_WIDTH
    view = p32.reshape(batch, seq, p32.shape[-1])

    def vec(name):
        return _const_spec((None, 1, w), lambda b, c: (layer, 0, 0)), lw[name]

    specs, args = [], []
    for spec, arr in (
        (pl.BlockSpec((None, tc, w), lambda b, c: (b, c, xr_col)), view),
        (pl.BlockSpec((None, tc, w), lambda b, c: (b, c, xr_col + 1)), view),
        (_const_spec((None, CONV_WIDTH, w), lambda b, c: (layer, 0, 0)), lw["conv_w"]),
        vec("conv_b"),
        (_const_spec((None, w, w), lambda b, c: (layer, 0, 0)), lw["w_rgate"]),
        vec("b_rgate"),
        (_const_spec((None, w, w), lambda b, c: (layer, 0, 0)), lw["w_igate"]),
        vec("b_igate"),
        vec("lru_lambda"),
    ):
        specs.append(spec)
        args.append(arr)
    rec, h_last = pl.pallas_call(
        _scan_body,
        grid=(batch, nc),
        in_specs=specs,
        out_specs=[
            pl.BlockSpec((None, tc, w), lambda b, c: (b, c, 0)),
            pl.BlockSpec((None, 1, w), lambda b, c: (b, 0, 0)),
        ],
        out_shape=[
            jax.ShapeDtypeStruct((batch, seq, w), F32),
            jax.ShapeDtypeStruct((batch, 1, w), F32),
        ],
        scratch_shapes=[
            pltpu.VMEM((tc + 8, w), F32),
            pltpu.VMEM((tc, w), F32),
            pltpu.VMEM((tc, w), F32),
            pltpu.VMEM((1, w), F32),
        ],
        compiler_params=_params("parallel", "arbitrary"),
        name="conv_rglru_scan",
    )(*args)
    return rec.reshape(batch * seq, w), h_last.reshape(batch, w)


def _step_body(xr_ref, yg_ref, sc_ref, h0_ref, cw_ref, cb_ref, wr_ref, br_ref, wi_ref, bi_ref, lam_ref,
               rec_ref, h_ref, conv_ref):
    w = xr_ref.shape[1]
    xr = xr_ref[...]
    xc = cb_ref[...] + xr * cw_ref[CONV_WIDTH - 1:CONV_WIDTH, :]
    for j in range(CONV_WIDTH - 1):
        xc = xc + sc_ref[:, j * w:(j + 1) * w] * cw_ref[j:j + 1, :]
    a, b = _lru_terms(xc, wr_ref, br_ref, wi_ref, bi_ref, lam_ref)
    h = a * h0_ref[...] + b
    h_ref[...] = h
    rec_ref[...] = _gelu(yg_ref[...]) * h
    conv_ref[:, 0:(CONV_WIDTH - 2) * w] = sc_ref[:, w:(CONV_WIDTH - 1) * w]
    conv_ref[:, (CONV_WIDTH - 2) * w:] = xr


def _step(p32, xr_col, state_conv, state_h, lw, layer):
    m = p32.shape[0]
    w = ATT_WIDTH
    sc = state_conv.reshape(state_conv.shape[0], m, (CONV_WIDTH - 1) * w)

    def vec(name):
        return _const_spec((None, 1, w), lambda i: (layer, 0, 0)), lw[name]

    specs, args = [], []
    for spec, arr in (
        (pl.BlockSpec((m, w), lambda i: (0, xr_col)), p32),
        (pl.BlockSpec((m, w), lambda i: (0, xr_col + 1)), p32),
        (pl.BlockSpec((None, m, (CONV_WIDTH - 1) * w), lambda i: (layer, 0, 0)), sc),
        (pl.BlockSpec((None, m, w), lambda i: (layer, 0, 0)), state_h),
        (_const_spec((None, CONV_WIDTH, w), lambda i: (layer, 0, 0)), lw["conv_w"]),
        vec("conv_b"),
        (_const_spec((None, w, w), lambda i: (layer, 0, 0)), lw["w_rgate"]),
        vec("b_rgate"),
        (_const_spec((None, w, w), lambda i: (layer, 0, 0)), lw["w_igate"]),
        vec("b_igate"),
        vec("lru_lambda"),
    ):
        specs.append(spec)
        args.append(arr)
    return pl.pallas_call(
        _step_body,
        grid=(1,),
        in_specs=specs,
        out_specs=[
            pl.BlockSpec((m, w), lambda i: (0, 0)),
            pl.BlockSpec((m, w), lambda i: (0, 0)),
            pl.BlockSpec((m, (CONV_WIDTH - 1) * w), lambda i: (0, 0)),
        ],
        out_shape=[
            jax.ShapeDtypeStruct((m, w), F32),
            jax.ShapeDtypeStruct((m, w), F32),
            jax.ShapeDtypeStruct((m, (CONV_WIDTH - 1) * w), F32),
        ],
        compiler_params=_params("arbitrary"),
        name="conv_rglru_step",
    )(*args)


def _sample_rows(cfg, step, qt_ref, knt_ref, vnt_ref, k_ref, v_ref, bias_ref, cnt_ref, o_ref):
    first_row, n_rows = cfg
    _, nh, hd, _ = k_ref.shape
    lane = lax.broadcasted_iota(jnp.int32, qt_ref.shape, 1)

    @pl.when(step == 0)
    def _():
        o_ref[...] = jnp.zeros_like(o_ref)

    for j in range(n_rows):
        mine = lane == first_row + step * n_rows + j

        def column(ref):
            return jnp.sum(jnp.where(mine, ref[...], 0.0), axis=-1, keepdims=True).reshape(nh, hd, 1)

        q = column(qt_ref) * Q_SCALE
        k_new = column(knt_ref)
        v_new = column(vnt_ref)
        s = jnp.sum(k_ref[j] * q, axis=1, keepdims=True) + bias_ref[...]
        s_new = jnp.sum(k_new * q, axis=1, keepdims=True)
        m = jnp.maximum(jnp.max(s, axis=-1, keepdims=True), s_new)
        p = jnp.exp(s - m) * cnt_ref[...]
        p_new = jnp.exp(s_new - m) * float(len(DILATED))
        l = jnp.sum(p, axis=-1, keepdims=True) + p_new
        acc = jnp.sum(p * v_ref[j], axis=-1, keepdims=True) + p_new * v_new
        o_ref[...] = jnp.where(mine, (acc / l).reshape(nh * hd, 1), o_ref[...])


def _linear_step(grid):
    def lin(*idx):
        out = 0
        for i, n in zip(idx, grid):
            out = out * n + i
        return out
    return lin


def _sample_operands(sample_args, grid):
    (first_row, n_rows), qkv_t, cache_kt, cache_vt, bias, cnt, layer = sample_args
    _, batch, nh, hd, w_buf = cache_kt.shape
    w = nh * hd
    win = bias.shape[-1]
    n_steps = 1
    for n in grid:
        n_steps *= n
    assert first_row % n_rows == 0 and first_row + n_rows * n_steps <= batch
    lin = _linear_step(grid)
    col = lambda c: _const_spec((w, batch), lambda *i: (c, 0))
    cache = pl.BlockSpec((None, n_rows, nh, hd, win),
                         lambda *i: (layer, first_row // n_rows + lin(*i), 0, 0, w_buf // win - 1))
    specs = [col(0), col(1), col(2), cache, cache,
             _const_spec((nh, 1, win), lambda *i: (0, 0, 0)), _const_spec((1, 1, win), lambda *i: (0, 0, 0))]
    args = [qkv_t, qkv_t, qkv_t, cache_kt, cache_vt, bias, cnt]
    return specs, args, pl.BlockSpec((w, batch), lambda *i: (0, 0)), jax.ShapeDtypeStruct((w, batch), F32)


def _sample_tables(slopes):
    dist = MAX_WINDOW - jnp.arange(MAX_WINDOW, dtype=jnp.int32)
    cnt = sum(((dist % dil == 0) & (dist <= win)).astype(F32) for win, dil in DILATED)
    bias = jnp.where(cnt > 0, -slopes[:, None] * dist.astype(F32)[None, :], NEG)
    return bias[:, None, :], cnt[None, None, :]


def _mix_body(att_ref, rec_ref, x_ref, ga_ref, gl_ref, w_ref, gp_ref, *rest, sample):
    if sample:
        _sample_rows(sample, pl.program_id(0), *rest[:7], rest[-1])
    out_ref = rest[-2] if sample else rest[-1]
    if len(att_ref.shape) == 3:
        att = jnp.concatenate([att_ref[p] for p in range(att_ref.shape[0])], axis=-1).astype(F32)
    else:
        att = att_ref[...]
    half = att.shape[1]
    att_n = _rms(att, ga_ref[...]).astype(BF16)
    rec_n = _rms(rec_ref[...], gl_ref[...]).astype(BF16)
    mixed = _dot(att_n, w_ref[0:half, :]) + _dot(rec_n, w_ref[half:, :])
    out_ref[...] = x_ref[...] + _rms(mixed, gp_ref[...])


def _mix(att, rec, x, lw, layer, tm, sample=None):
    m, d = x.shape
    w = ATT_WIDTH
    row = lambda width: pl.BlockSpec((tm, width), lambda i: (i, 0))
    att_spec = row(w) if att.ndim == 2 else pl.BlockSpec((att.shape[0], tm, 128), lambda i: (0, i, 0))
    in_specs = [
        att_spec, row(w), row(d),
        _const_spec((None, 1, w), lambda i: (layer, 0, 0)),
        _const_spec((None, 1, w), lambda i: (layer, 0, 0)),
        _const_spec((None, 2 * w, d), lambda i: (layer, 0, 0)),
        _const_spec((None, 1, d), lambda i: (layer, 0, 0)),
    ]
    args = [att, rec, x, lw["g_att_out"], lw["g_lru_out"], lw["w_out"], lw["g_post_mix"]]
    out_specs = [row(d)]
    out_shape = [jax.ShapeDtypeStruct((m, d), F32)]
    if sample:
        sa_specs, sa_args, sa_out_spec, sa_out_shape = _sample_operands(sample, (m // tm,))
        in_specs += sa_specs
        args += sa_args
        out_specs.append(sa_out_spec)
        out_shape.append(sa_out_shape)
    return pl.pallas_call(
        functools.partial(_mix_body, sample=sample[0] if sample else None),
        grid=(m // tm,),
        in_specs=in_specs,
        out_specs=out_specs,
        out_shape=out_shape,
        compiler_params=_params("arbitrary" if sample else "parallel"),
        name="mix_out",
    )(*args)


def _ffn_body(x_ref, g1_ref, wg_ref, wu_ref, wd_ref, g2_ref, *rest, sample):
    if sample:
        _sample_rows(sample, pl.program_id(0), *rest[:7], rest[-1])
    out_ref = rest[-2] if sample else rest[-1]
    x = x_ref[...]
    h = _rms(x, g1_ref[...]).astype(BF16)
    acc = None
    for f0, f1 in FFN_CHUNKS:
        gate = _dot(h, wg_ref[:, f0:f1])
        up = _dot(h, wu_ref[:, f0:f1])
        act = (gate * jax.nn.sigmoid(gate) * up).astype(BF16)
        part = _dot(act, wd_ref[f0:f1, :])
        acc = part if acc is None else acc + part
    out_ref[...] = x + _rms(acc, g2_ref[...])


def _ffn(x, lw, layer, tm, sample=None):
    m, d = x.shape
    f = lw["w_ffn_gate"].shape[-1]
    assert FFN_CHUNKS[-1][1] == f
    in_specs = [
        pl.BlockSpec((tm, d), lambda i: (i, 0)),
        _const_spec((None, 1, d), lambda i: (layer, 0, 0)),
        _const_spec((None, d, f), lambda i: (layer, 0, 0)),
        _const_spec((None, d, f), lambda i: (layer, 0, 0)),
        _const_spec((None, f, d), lambda i: (layer, 0, 0)),
        _const_spec((None, 1, d), lambda i: (layer, 0, 0)),
    ]
    args = [x, lw["g_pre_ffn"], lw["w_ffn_gate"], lw["w_ffn_up"], lw["w_ffn_down"], lw["g_post_ffn"]]
    out_specs = [pl.BlockSpec((tm, d), lambda i: (i, 0))]
    out_shape = [jax.ShapeDtypeStruct((m, d), F32)]
    if sample:
        sa_specs, sa_args, sa_out_spec, sa_out_shape = _sample_operands(sample, (m // tm,))
        in_specs += sa_specs
        args += sa_args
        out_specs.append(sa_out_spec)
        out_shape.append(sa_out_shape)
    return pl.pallas_call(
        functools.partial(_ffn_body, sample=sample[0] if sample else None),
        grid=(m // tm,),
        in_specs=in_specs,
        out_specs=out_specs,
        out_shape=out_shape,
        compiler_params=_params("arbitrary" if sample else "parallel"),
        name="ffn",
    )(*args)


def _block_diag(w):
    depth, nblk, c, _ = w.shape
    eye = jnp.eye(nblk, dtype=w.dtype)
    return jnp.einsum("lncd,nm->lncmd", w, eye).reshape(depth, nblk * c, nblk * c)


def _heads_from_slabs(slabs, first, batch, seq, n_keep):
    t = slabs.reshape(slabs.shape[0], batch, seq, 128)[first:first + N_PAIRS, :, seq - n_keep:, :]
    return t.transpose(1, 2, 0, 3).reshape(batch, n_keep, N_HEADS, HEAD_DIM)


def kernel(x_prompt, x_sample, cache_k, cache_v, state_conv, state_h, g_pre_mix, g_post_mix, w_in, w_out, conv_w, conv_b, w_rgate, b_rgate, w_igate, b_igate, lru_lambda, g_att_out, g_lru_out, g_pre_ffn, g_post_ffn, w_ffn_gate, w_ffn_up, w_ffn_down):
    batch, seq, d_model = x_prompt.shape
    dec_batch, dec_seq, _ = x_sample.shape
    depth = w_in.shape[0]
    w = ATT_WIDTH
    assert dec_seq == 1 and seq % ATT_CHUNK == 0 and cache_k.shape[2] % MAX_WINDOW == 0
    n_keep = min(MAX_WINDOW, seq)

    vec = lambda a: a.reshape(depth, 1, a.shape[-1])
    lw = dict(
        g_pre_mix=vec(g_pre_mix), g_post_mix=vec(g_post_mix), g_att_out=vec(g_att_out), g_lru_out=vec(g_lru_out),
        g_pre_ffn=vec(g_pre_ffn), g_post_ffn=vec(g_post_ffn), conv_b=vec(conv_b), b_rgate=vec(b_rgate),
        b_igate=vec(b_igate), lru_lambda=vec(lru_lambda), conv_w=conv_w,
        w_in=w_in.astype(BF16), w_out=w_out.astype(BF16),
        w_rgate=_block_diag(w_rgate).astype(BF16), w_igate=_block_diag(w_igate).astype(BF16),
        w_ffn_gate=w_ffn_gate.astype(BF16), w_ffn_up=w_ffn_up.astype(BF16), w_ffn_down=w_ffn_down.astype(BF16),
    )

    slopes = 2.0 ** (-(8.0 / N_HEADS) * jnp.arange(1, N_HEADS + 1, dtype=F32))
    prompt_bias = _attn_bias(slopes)
    sample_bias, sample_cnt = _sample_tables(slopes)
    cache_kt = cache_k.transpose(0, 1, 3, 4, 2)
    cache_vt = cache_v.transpose(0, 1, 3, 4, 2)

    tm_p = 512
    n_tiles = (batch * seq) // tm_p
    n_att_steps = batch * N_PAIRS * (seq // ATT_CHUNK)
    hosts = [(0, 1), (n_tiles, 2), (n_tiles + 2 * n_att_steps, 1)]
    assert 2 * n_tiles + 2 * n_att_steps == dec_batch
    xp = x_prompt.reshape(batch * seq, d_model)
    xs = x_sample.reshape(dec_batch, d_model)
    kp, vp, cp, hp, ks, vs, cs, hs = [], [], [], [], [], [], [], []
    for l in range(depth):
        (s32,) = _proj(xs, lw["g_pre_mix"], lw["w_in"], l, dec_batch, False)
        sample = (s32[:, :3 * w].T, cache_kt, cache_vt, sample_bias, sample_cnt, l)
        slabs, rg, att_s0 = _proj(xp, lw["g_pre_mix"], lw["w_in"], l, tm_p, True, (hosts[0],) + sample)
        att, att_s1 = _attention(slabs, prompt_bias, batch, seq, (hosts[1],) + sample)
        rec, h_last = _scan(rg, 0, lw, l, batch, seq)
        (xp,) = _mix(att, rec, xp, lw, l, tm_p)
        xp, att_s2 = _ffn(xp, lw, l, tm_p, (hosts[2],) + sample)
        kp.append(_heads_from_slabs(slabs, N_PAIRS, batch, seq, n_keep))
        vp.append(_heads_from_slabs(slabs, 2 * N_PAIRS, batch, seq, n_keep))
        cp.append(rg.reshape(batch, seq, 2 * w)[:, seq - (CONV_WIDTH - 1):, :w])
        hp.append(h_last)
        att_s = (att_s0 + att_s1 + att_s2).T
        rec_s, h_s, conv_s = _step(s32, 3, state_conv, state_h, lw, l)
        (xs,) = _mix(att_s, rec_s, xs, lw, l, dec_batch)
        (xs,) = _ffn(xs, lw, l, dec_batch)
        ks.append(s32[:, w:2 * w].reshape(dec_batch, 1, N_HEADS, HEAD_DIM))
        vs.append(s32[:, 2 * w:3 * w].reshape(dec_batch, 1, N_HEADS, HEAD_DIM))
        cs.append(conv_s.reshape(dec_batch, CONV_WIDTH - 1, w))
        hs.append(h_s)
    return (xp.reshape(batch, seq, d_model), xs.reshape(dec_batch, 1, d_model),
            jnp.stack(kp), jnp.stack(vp), jnp.stack(cp), jnp.stack(hp),
            jnp.stack(ks), jnp.stack(vs), jnp.stack(cs), jnp.stack(hs))
```

```python
import functools

import jax
import jax.numpy as jnp
from jax import lax
from jax.experimental import pallas as pl
from jax.experimental.pallas import tpu as pltpu

F32 = jnp.float32
BF16 = jnp.bfloat16

N_HEADS = 8
HEAD_DIM = 64
ATT_WIDTH = N_HEADS * HEAD_DIM
N_PAIRS = N_HEADS // 2
CONV_WIDTH = 4
LRU_C = 8.0
DILATED = ((128, 1), (512, 4), (2048, 16))
MAX_WINDOW = 2048
BLK = 128
EPS = 1e-6
NEG = -1e30
Q_SCALE = HEAD_DIM ** -0.5
LOG2E = 1.4426950408889634

VMEM_LIMIT_BYTES = 56 * 1024 * 1024
FFN_CHUNKS = ((0, 1536), (1536, 2816))
SCAN_CHUNK = 512
ATT_CHUNK = BLK * DILATED[-1][1]
ATT_GROUP = 4


def _params(*sem):
    return pltpu.CompilerParams(dimension_semantics=sem, vmem_limit_bytes=VMEM_LIMIT_BYTES)


def _const_spec(shape, index):
    return pl.BlockSpec(shape, index, pipeline_mode=pl.Buffered(1))


def _rms(x, g):
    return x * lax.rsqrt(jnp.mean(x * x, axis=-1, keepdims=True) + EPS) * g


def _dot(a, b):
    return jnp.dot(a, b, preferred_element_type=F32)


def _proj_body(x_ref, g_ref, w_ref, *rest, slabs, sample):
    if sample:
        sample_refs, out_refs = rest[:7], rest[7:-1]
        _sample_rows(sample, pl.program_id(0), *sample_refs, rest[-1])
    else:
        out_refs = rest
    h = _rms(x_ref[...], g_ref[...]).astype(BF16)
    n = w_ref.shape[1]
    n_att = 3 * ATT_WIDTH
    for c in range(n // ATT_WIDTH):
        lo, hi = c * ATT_WIDTH, (c + 1) * ATT_WIDTH
        pc = _dot(h, w_ref[:, lo:hi])
        if not slabs:
            out_refs[0][:, lo:hi] = pc
        elif lo >= n_att:
            out_refs[1][:, lo - n_att:hi - n_att] = pc
        else:
            for p in range(N_PAIRS):
                part = pc[:, p * 128:(p + 1) * 128]
                out_refs[0][c * N_PAIRS + p] = part * (Q_SCALE * LOG2E) if c == 0 else part


def _proj(x, g, w, layer, tm, slabs, sample=None):
    m, d = x.shape
    n = w.shape[-1]
    if slabs:
        n_rest = n - 3 * ATT_WIDTH
        out_specs = [pl.BlockSpec((3 * N_PAIRS, tm, 128), lambda i: (0, i, 0)),
                     pl.BlockSpec((tm, n_rest), lambda i: (i, 0))]
        out_shape = [jax.ShapeDtypeStruct((3 * N_PAIRS, m, 128), F32),
                     jax.ShapeDtypeStruct((m, n_rest), F32)]
    else:
        out_specs = [pl.BlockSpec((tm, n), lambda i: (i, 0))]
        out_shape = [jax.ShapeDtypeStruct((m, n), F32)]
    in_specs = [
        pl.BlockSpec((tm, d), lambda i: (i, 0)),
        _const_spec((None, 1, d), lambda i: (layer, 0, 0)),
        _const_spec((None, d, n), lambda i: (layer, 0, 0)),
    ]
    args = [x, g, w]
    if sample:
        sa_specs, sa_args, sa_out_spec, sa_out_shape = _sample_operands(sample, (m // tm,))
        in_specs += sa_specs
        args += sa_args
        out_specs.append(sa_out_spec)
        out_shape.append(sa_out_shape)
    return pl.pallas_call(
        functools.partial(_proj_body, slabs=slabs, sample=sample[0] if sample else None),
        grid=(m // tm,),
        in_specs=in_specs,
        out_specs=out_specs,
        out_shape=out_shape,
        compiler_params=_params("arbitrary" if sample else "parallel"),
        name="proj_in",
    )(*args)


def _attn_body(q_ref, k_ref, v_ref, kp_ref, vp_ref, bias_ref, *rest, sample):
    if sample:
        grid = tuple(pl.num_programs(a) for a in range(3))
        step = _linear_step(grid)(*(pl.program_id(a) for a in range(3)))
        _sample_rows(sample, step, *rest[:7], rest[-5])
        rest = rest[7:-5] + rest[-4:]
    o_ref, t_a, t_b, m_a, m_b = rest
    t_s = (t_a, t_b)
    m_s = (m_a, m_b)
    c = pl.program_id(2)
    C = q_ref.shape[0]
    nt = (((1,), (1,)), ((), ()))
    first_head = lax.broadcasted_iota(jnp.int32, (1, 128), 1) < HEAD_DIM
    heads = (first_head, jnp.logical_not(first_head))

    def rows(start, d):
        return pl.ds(start, BLK) if d == 1 else pl.ds(start, BLK, stride=d)

    def prev_own(cur_ref, prev_ref, st, d, head_of_chunk):
        own = cur_ref[rows(st, d), :]
        prev = prev_ref[rows(C - BLK * d + st, d), :] if head_of_chunk else cur_ref[rows(st - BLK * d, d), :]
        return jnp.concatenate([prev.astype(BF16), own.astype(BF16)], axis=0)

    def run_blocks(bi, d, blocks, init):
        g = len(blocks)
        qs = [q_ref[rows(st, d), :] for st, _ in blocks]
        ks = [prev_own(k_ref, kp_ref, st, d, hd) for st, hd in blocks]
        vs = [prev_own(v_ref, vp_ref, st, d, hd) for st, hd in blocks]
        variant = [(c == 0).astype(jnp.int32) if hd else 0 for _, hd in blocks]
        ss = []
        for i in range(g):
            for h in range(2):
                qm = jnp.where(heads[h], qs[i], 0.0).astype(BF16)
                ss.append(lax.dot_general(qm, ks[i], nt, preferred_element_type=F32) + bias_ref[bi, h, variant[i]])
        ms = [jnp.max(a, axis=-1, keepdims=True) for a in ss]
        ps = [jnp.exp2(a - m).astype(BF16) for a, m in zip(ss, ms)]
        ts = [_dot(ps[2 * i + h], jnp.where(heads[h], vs[i], jnp.ones_like(vs[i])))
              for i in range(g) for h in range(2)]
        for i, (st, _) in enumerate(blocks):
            r = rows(st, d)
            for h in range(2):
                m_t = jnp.broadcast_to(ms[2 * i + h], (BLK, 128))
                t_t = ts[2 * i + h]
                if init:
                    m_s[h][r, :] = m_t
                    t_s[h][r, :] = t_t
                else:
                    m_old = m_s[h][r, :]
                    m_new = jnp.maximum(m_old, m_t)
                    m_s[h][r, :] = m_new
                    t_s[h][r, :] = t_s[h][r, :] * jnp.exp2(m_old - m_new) + t_t * jnp.exp2(m_t - m_new)

    n_groups = C // (BLK * ATT_GROUP)
    order = sorted(range(len(DILATED)), key=lambda b: -DILATED[b][1])
    for pos, bi in enumerate(order):
        d = DILATED[bi][1]
        per_res = C // (BLK * d)
        init = pos == 0

        def group(gi, bi=bi, d=d, per_res=per_res, init=init):
            blocks = []
            for u in range(ATT_GROUP):
                j = gi * ATT_GROUP + u
                if per_res <= ATT_GROUP:
                    n = u % per_res
                    blocks.append((n * (BLK * d) + j // per_res, n == 0))
                else:
                    blocks.append((j * (BLK * d), isinstance(j, int) and j == 0))
            run_blocks(bi, d, blocks, init)

        if per_res <= ATT_GROUP:
            assert ATT_GROUP % per_res == 0
            lax.fori_loop(0, n_groups, lambda gi, carry: (group(gi), carry)[1], 0)
        else:
            assert per_res == C // BLK
            group(0)
            lax.fori_loop(1, n_groups, lambda gi, carry: (group(gi), carry)[1], 0)

    ta = t_a[...]
    tb = t_b[...]
    o_ref[...] = jnp.where(first_head, ta / pltpu.roll(ta, HEAD_DIM, 1),
                           tb / pltpu.roll(tb, HEAD_DIM, 1)).astype(o_ref.dtype)


def _attention(slabs, bias, batch, seq, sample=None):
    C = ATT_CHUNK
    nch = seq // C
    grid = (batch, N_PAIRS, nch)

    def slab(off, back=0):
        return pl.BlockSpec((None, C, 128), lambda b, p, c: (off + p, b * nch + jnp.maximum(c - back, 0), 0))

    state = pltpu.VMEM((C, 128), F32)
    in_specs = [slab(0), slab(N_PAIRS), slab(2 * N_PAIRS), slab(N_PAIRS, 1), slab(2 * N_PAIRS, 1),
                pl.BlockSpec((len(DILATED), None, 2, 2, BLK, 2 * BLK), lambda b, p, c: (0, p, 0, 0, 0, 0))]
    args = [slabs, slabs, slabs, slabs, slabs, bias]
    out_specs = [pl.BlockSpec((None, C, 128), lambda b, p, c: (p, b * nch + c, 0))]
    out_shape = [jax.ShapeDtypeStruct((N_PAIRS, batch * seq, 128), BF16)]
    if sample:
        sa_specs, sa_args, sa_out_spec, sa_out_shape = _sample_operands(sample, grid)
        in_specs += sa_specs
        args += sa_args
        out_specs.append(sa_out_spec)
        out_shape.append(sa_out_shape)
    return pl.pallas_call(
        functools.partial(_attn_body, sample=sample[0] if sample else None),
        grid=grid,
        in_specs=in_specs,
        out_specs=out_specs,
        out_shape=out_shape,
        scratch_shapes=[state, state, state, state],
        compiler_params=_params(*(("arbitrary",) * 3 if sample else ("parallel", "parallel", "arbitrary"))),
        name="attn",
    )(*args)


def _attn_bias(slopes):
    i = jnp.arange(BLK)[:, None]
    j = jnp.arange(2 * BLK)[None, :]
    steps = i + BLK - j
    ok = (steps >= 0) & (steps <= BLK)
    out = []
    for _, dil in DILATED:
        b = (-LOG2E * slopes)[:, None, None] * (dil * steps).astype(F32)[None]
        out.append(jnp.stack([jnp.where(m[None], b, NEG) for m in (ok, ok & (j >= BLK))], axis=1))
    return jnp.stack(out).reshape(len(DILATED), N_PAIRS, 2, 2, BLK, 2 * BLK)


def _softplus(x):
    return jnp.maximum(x, 0.0) + jnp.log1p(jnp.exp(-jnp.abs(x)))


def _gelu(x):
    return 0.5 * x * (1.0 + jnp.tanh(0.7978845608028654 * (x + 0.044715 * (x * x * x))))


def _sigmoid(x):
    return 0.5 * jnp.tanh(0.5 * x) + 0.5


def _lru_terms(xc, wr_ref, br_ref, wi_ref, bi_ref, lam_ref):
    xb = xc.astype(BF16)
    r = _sigmoid(_dot(xb, wr_ref[...]) + br_ref[...])
    i = _sigmoid(_dot(xb, wi_ref[...]) + bi_ref[...])
    log_a = (-LRU_C * _softplus(-lam_ref[...])) * r
    a = jnp.exp(log_a)
    b = jnp.sqrt(-jnp.tanh(log_a) * (a * a + 1.0)) * (i * xc)
    return a, b


def _scan_body(xr_ref, yg_ref, cw_ref, cb_ref, wr_ref, br_ref, wi_ref, bi_ref, lam_ref, *rest, sample):
    if sample:
        step = pl.program_id(0) * pl.num_programs(1) + pl.program_id(1)
        _sample_rows(sample, step, *rest[:7], rest[9])
        rest = rest[7:9] + rest[10:]
    rec_ref, hl_ref, xbuf, a_s, b_s, h_s = rest
    tc = xr_ref.shape[0]

    @pl.when(pl.program_id(1) == 0)
    def _():
        xbuf[0:8, :] = jnp.zeros((8, xbuf.shape[1]), F32)
        h_s[...] = jnp.zeros_like(h_s)

    xbuf[8:, :] = xr_ref[...]
    xc = cb_ref[...]
    for j in range(CONV_WIDTH):
        xc = xc + xbuf[pl.ds(8 - (CONV_WIDTH - 1) + j, tc), :] * cw_ref[j:j + 1, :]
    xbuf[0:8, :] = xbuf[tc:tc + 8, :]

    a, b = _lru_terms(xc, wr_ref, br_ref, wi_ref, bi_ref, lam_ref)
    a_s[...] = a
    b_s[...] = b
    row = lax.broadcasted_iota(jnp.int32, (8, a.shape[1]), 0)

    def group(g, h):
        r0 = pl.multiple_of(g * 8, 8)
        ca = a_s[pl.ds(r0, 8), :]
        cb = b_s[pl.ds(r0, 8), :]
        for s in (1, 2, 4):
            keep = row >= s
            cb = jnp.where(keep, ca * pltpu.roll(cb, s, 0) + cb, cb)
            ca = jnp.where(keep, ca * pltpu.roll(ca, s, 0), ca)
        h8 = ca * h + cb
        b_s[pl.ds(r0, 8), :] = h8
        return h8[7:8, :]

    h = lax.fori_loop(0, tc // 8, group, h_s[...], unroll=8)
    h_s[...] = h
    hl_ref[...] = h
    rec_ref[...] = _gelu(yg_ref[...]) * b_s[...]


def _scan(p32, xr_col, lw, layer, batch, seq, sample=None):
    tc = SCAN_CHUNK
    nc = seq // tc
    w = ATT_WIDTH
    view = p32.reshape(batch, seq, p32.shape[-1])

    def vec(name):
        return _const_spec((None, 1, w), lambda b, c: (layer, 0, 0)), lw[name]

    specs, args = [], []
    for spec, arr in (
        (pl.BlockSpec((None, tc, w), lambda b, c: (b, c, xr_col)), view),
        (pl.BlockSpec((None, tc, w), lambda b, c: (b, c, xr_col + 1)), view),
        (_const_spec((None, CONV_WIDTH, w), lambda b, c: (layer, 0, 0)), lw["conv_w"]),
        vec("conv_b"),
        (_const_spec((None, w, w), lambda b, c: (layer, 0, 0)), lw["w_rgate"]),
        vec("b_rgate"),
        (_const_spec((None, w, w), lambda b, c: (layer, 0, 0)), lw["w_igate"]),
        vec("b_igate"),
        vec("lru_lambda"),
    ):
        specs.append(spec)
        args.append(arr)
    out_specs = [
        pl.BlockSpec((None, tc, w), lambda b, c: (b, c, 0)),
        pl.BlockSpec((None, 1, w), lambda b, c: (b, 0, 0)),
    ]
    out_shape = [
        jax.ShapeDtypeStruct((batch, seq, w), F32),
        jax.ShapeDtypeStruct((batch, 1, w), F32),
    ]
    if sample:
        sa_specs, sa_args, sa_out_spec, sa_out_shape = _sample_operands(sample, (batch, nc))
        specs += sa_specs
        args += sa_args
        out_specs.append(sa_out_spec)
        out_shape.append(sa_out_shape)
    rec, h_last, *sa_out = pl.pallas_call(
        functools.partial(_scan_body, sample=sample[0] if sample else None),
        grid=(batch, nc),
        in_specs=specs,
        out_specs=out_specs,
        out_shape=out_shape,
        scratch_shapes=[
            pltpu.VMEM((tc + 8, w), F32),
            pltpu.VMEM((tc, w), F32),
            pltpu.VMEM((tc, w), F32),
            pltpu.VMEM((1, w), F32),
        ],
        compiler_params=_params(*(("arbitrary", "arbitrary") if sample else ("parallel", "arbitrary"))),
        name="conv_rglru_scan",
    )(*args)
    return (rec.reshape(batch * seq, w), h_last.reshape(batch, w), *sa_out)


def _step_body(xr_ref, yg_ref, sc_ref, h0_ref, cw_ref, cb_ref, wr_ref, br_ref, wi_ref, bi_ref, lam_ref,
               rec_ref, h_ref, conv_ref):
    w = xr_ref.shape[1]
    xr = xr_ref[...]
    xc = cb_ref[...] + xr * cw_ref[CONV_WIDTH - 1:CONV_WIDTH, :]
    for j in range(CONV_WIDTH - 1):
        xc = xc + sc_ref[:, j * w:(j + 1) * w] * cw_ref[j:j + 1, :]
    a, b = _lru_terms(xc, wr_ref, br_ref, wi_ref, bi_ref, lam_ref)
    h = a * h0_ref[...] + b
    h_ref[...] = h
    rec_ref[...] = _gelu(yg_ref[...]) * h
    conv_ref[:, 0:(CONV_WIDTH - 2) * w] = sc_ref[:, w:(CONV_WIDTH - 1) * w]
    conv_ref[:, (CONV_WIDTH - 2) * w:] = xr


def _step(p32, xr_col, state_conv, state_h, lw, layer):
    m = p32.shape[0]
    w = ATT_WIDTH
    sc = state_conv.reshape(state_conv.shape[0], m, (CONV_WIDTH - 1) * w)

    def vec(name):
        return _const_spec((None, 1, w), lambda i: (layer, 0, 0)), lw[name]

    specs, args = [], []
    for spec, arr in (
        (pl.BlockSpec((m, w), lambda i: (0, xr_col)), p32),
        (pl.BlockSpec((m, w), lambda i: (0, xr_col + 1)), p32),
        (pl.BlockSpec((None, m, (CONV_WIDTH - 1) * w), lambda i: (layer, 0, 0)), sc),
        (pl.BlockSpec((None, m, w), lambda i: (layer, 0, 0)), state_h),
        (_const_spec((None, CONV_WIDTH, w), lambda i: (layer, 0, 0)), lw["conv_w"]),
        vec("conv_b"),
        (_const_spec((None, w, w), lambda i: (layer, 0, 0)), lw["w_rgate"]),
        vec("b_rgate"),
        (_const_spec((None, w, w), lambda i: (layer, 0, 0)), lw["w_igate"]),
        vec("b_igate"),
        vec("lru_lambda"),
    ):
        specs.append(spec)
        args.append(arr)
    return pl.pallas_call(
        _step_body,
        grid=(1,),
        in_specs=specs,
        out_specs=[
            pl.BlockSpec((m, w), lambda i: (0, 0)),
            pl.BlockSpec((m, w), lambda i: (0, 0)),
            pl.BlockSpec((m, (CONV_WIDTH - 1) * w), lambda i: (0, 0)),
        ],
        out_shape=[
            jax.ShapeDtypeStruct((m, w), F32),
            jax.ShapeDtypeStruct((m, w), F32),
            jax.ShapeDtypeStruct((m, (CONV_WIDTH - 1) * w), F32),
        ],
        compiler_params=_params("arbitrary"),
        name="conv_rglru_step",
    )(*args)


def _sample_rows(cfg, step, qt_ref, knt_ref, vnt_ref, k_ref, v_ref, bias_ref, cnt_ref, o_ref):
    first_row, n_rows = cfg
    _, nh, hd, _ = k_ref.shape
    lane = lax.broadcasted_iota(jnp.int32, qt_ref.shape, 1)

    @pl.when(step == 0)
    def _():
        o_ref[...] = jnp.zeros_like(o_ref)

    for j in range(n_rows):
        mine = lane == first_row + step * n_rows + j

        def column(ref):
            return jnp.sum(jnp.where(mine, ref[...], 0.0), axis=-1, keepdims=True).reshape(nh, hd, 1)

        q = column(qt_ref) * Q_SCALE
        k_new = column(knt_ref)
        v_new = column(vnt_ref)
        s = jnp.sum(k_ref[j] * q, axis=1, keepdims=True) + bias_ref[...]
        s_new = jnp.sum(k_new * q, axis=1, keepdims=True)
        m = jnp.maximum(jnp.max(s, axis=-1, keepdims=True), s_new)
        p = jnp.exp(s - m) * cnt_ref[...]
        p_new = jnp.exp(s_new - m) * float(len(DILATED))
        l = jnp.sum(p, axis=-1, keepdims=True) + p_new
        acc = jnp.sum(p * v_ref[j], axis=-1, keepdims=True) + p_new * v_new
        o_ref[...] = jnp.where(mine, (acc / l).reshape(nh * hd, 1), o_ref[...])


def _linear_step(grid):
    def lin(*idx):
        out = 0
        for i, n in zip(idx, grid):
            out = out * n + i
        return out
    return lin


def _sample_operands(sample_args, grid):
    (first_row, n_rows), qkv_t, cache_kt, cache_vt, bias, cnt, layer = sample_args
    _, batch, nh, hd, w_buf = cache_kt.shape
    w = nh * hd
    win = bias.shape[-1]
    n_steps = 1
    for n in grid:
        n_steps *= n
    assert first_row % n_rows == 0 and first_row + n_rows * n_steps <= batch
    lin = _linear_step(grid)
    col = lambda c: _const_spec((w, batch), lambda *i: (c, 0))
    cache = pl.BlockSpec((None, n_rows, nh, hd, win),
                         lambda *i: (layer, first_row // n_rows + lin(*i), 0, 0, w_buf // win - 1))
    specs = [col(0), col(1), col(2), cache, cache,
             _const_spec((nh, 1, win), lambda *i: (0, 0, 0)), _const_spec((1, 1, win), lambda *i: (0, 0, 0))]
    args = [qkv_t, qkv_t, qkv_t, cache_kt, cache_vt, bias, cnt]
    return specs, args, pl.BlockSpec((w, batch), lambda *i: (0, 0)), jax.ShapeDtypeStruct((w, batch), F32)


def _sample_tables(slopes):
    dist = MAX_WINDOW - jnp.arange(MAX_WINDOW, dtype=jnp.int32)
    cnt = sum(((dist % dil == 0) & (dist <= win)).astype(F32) for win, dil in DILATED)
    bias = jnp.where(cnt > 0, -slopes[:, None] * dist.astype(F32)[None, :], NEG)
    return bias[:, None, :], cnt[None, None, :]


def _mix_body(att_ref, rec_ref, x_ref, ga_ref, gl_ref, w_ref, gp_ref, *rest, sample):
    if sample:
        _sample_rows(sample, pl.program_id(0), *rest[:7], rest[-1])
    out_ref = rest[-2] if sample else rest[-1]
    if len(att_ref.shape) == 3:
        att = jnp.concatenate([att_ref[p] for p in range(att_ref.shape[0])], axis=-1).astype(F32)
    else:
        att = att_ref[...]
    half = att.shape[1]
    att_n = _rms(att, ga_ref[...]).astype(BF16)
    rec_n = _rms(rec_ref[...], gl_ref[...]).astype(BF16)
    mixed = _dot(att_n, w_ref[0:half, :]) + _dot(rec_n, w_ref[half:, :])
    out_ref[...] = x_ref[...] + _rms(mixed, gp_ref[...])


def _mix(att, rec, x, lw, layer, tm, sample=None):
    m, d = x.shape
    w = ATT_WIDTH
    row = lambda width: pl.BlockSpec((tm, width), lambda i: (i, 0))
    att_spec = row(w) if att.ndim == 2 else pl.BlockSpec((att.shape[0], tm, 128), lambda i: (0, i, 0))
    in_specs = [
        att_spec, row(w), row(d),
        _const_spec((None, 1, w), lambda i: (layer, 0, 0)),
        _const_spec((None, 1, w), lambda i: (layer, 0, 0)),
        _const_spec((None, 2 * w, d), lambda i: (layer, 0, 0)),
        _const_spec((None, 1, d), lambda i: (layer, 0, 0)),
    ]
    args = [att, rec, x, lw["g_att_out"], lw["g_lru_out"], lw["w_out"], lw["g_post_mix"]]
    out_specs = [row(d)]
    out_shape = [jax.ShapeDtypeStruct((m, d), F32)]
    if sample:
        sa_specs, sa_args, sa_out_spec, sa_out_shape = _sample_operands(sample, (m // tm,))
        in_specs += sa_specs
        args += sa_args
        out_specs.append(sa_out_spec)
        out_shape.append(sa_out_shape)
    return pl.pallas_call(
        functools.partial(_mix_body, sample=sample[0] if sample else None),
        grid=(m // tm,),
        in_specs=in_specs,
        out_specs=out_specs,
        out_shape=out_shape,
        compiler_params=_params("arbitrary" if sample else "parallel"),
        name="mix_out",
    )(*args)


def _ffn_body(x_ref, g1_ref, wg_ref, wu_ref, wd_ref, g2_ref, *rest, sample):
    if sample:
        _sample_rows(sample, pl.program_id(0), *rest[:7], rest[-1])
    out_ref = rest[-2] if sample else rest[-1]
    x = x_ref[...]
    h = _rms(x, g1_ref[...]).astype(BF16)
    acc = None
    for f0, f1 in FFN_CHUNKS:
        gate = _dot(h, wg_ref[:, f0:f1])
        up = _dot(h, wu_ref[:, f0:f1])
        act = (gate * jax.nn.sigmoid(gate) * up).astype(BF16)
        part = _dot(act, wd_ref[f0:f1, :])
        acc = part if acc is None else acc + part
    out_ref[...] = x + _rms(acc, g2_ref[...])


def _ffn(x, lw, layer, tm, sample=None):
    m, d = x.shape
    f = lw["w_ffn_gate"].shape[-1]
    assert FFN_CHUNKS[-1][1] == f
    in_specs = [
        pl.BlockSpec((tm, d), lambda i: (i, 0)),
        _const_spec((None, 1, d), lambda i: (layer, 0, 0)),
        _const_spec((None, d, f), lambda i: (layer, 0, 0)),
        _const_spec((None, d, f), lambda i: (layer, 0, 0)),
        _const_spec((None, f, d), lambda i: (layer, 0, 0)),
        _const_spec((None, 1, d), lambda i: (layer, 0, 0)),
    ]
    args = [x, lw["g_pre_ffn"], lw["w_ffn_gate"], lw["w_ffn_up"], lw["w_ffn_down"], lw["g_post_ffn"]]
    out_specs = [pl.BlockSpec((tm, d), lambda i: (i, 0))]
    out_shape = [jax.ShapeDtypeStruct((m, d), F32)]
    if sample:
        sa_specs, sa_args, sa_out_spec, sa_out_shape = _sample_operands(sample, (m // tm,))
        in_specs += sa_specs
        args += sa_args
        out_specs.append(sa_out_spec)
        out_shape.append(sa_out_shape)
    return pl.pallas_call(
        functools.partial(_ffn_body, sample=sample[0] if sample else None),
        grid=(m // tm,),
        in_specs=in_specs,
        out_specs=out_specs,
        out_shape=out_shape,
        compiler_params=_params("arbitrary" if sample else "parallel"),
        name="ffn",
    )(*args)


def _block_diag(w):
    depth, nblk, c, _ = w.shape
    eye = jnp.eye(nblk, dtype=w.dtype)
    return jnp.einsum("lncd,nm->lncmd", w, eye).reshape(depth, nblk * c, nblk * c)


def _heads_from_slabs(slabs, first, batch, seq, n_keep):
    t = slabs.reshape(slabs.shape[0], batch, seq, 128)[first:first + N_PAIRS, :, seq - n_keep:, :]
    return t.transpose(1, 2, 0, 3).reshape(batch, n_keep, N_HEADS, HEAD_DIM)


def kernel(x_prompt, x_sample, cache_k, cache_v, state_conv, state_h, g_pre_mix, g_post_mix, w_in, w_out, conv_w, conv_b, w_rgate, b_rgate, w_igate, b_igate, lru_lambda, g_att_out, g_lru_out, g_pre_ffn, g_post_ffn, w_ffn_gate, w_ffn_up, w_ffn_down):
    batch, seq, d_model = x_prompt.shape
    dec_batch, dec_seq, _ = x_sample.shape
    depth = w_in.shape[0]
    w = ATT_WIDTH
    assert dec_seq == 1 and seq % ATT_CHUNK == 0 and cache_k.shape[2] % MAX_WINDOW == 0
    n_keep = min(MAX_WINDOW, seq)

    vec = lambda a: a.reshape(depth, 1, a.shape[-1])
    lw = dict(
        g_pre_mix=vec(g_pre_mix), g_post_mix=vec(g_post_mix), g_att_out=vec(g_att_out), g_lru_out=vec(g_lru_out),
        g_pre_ffn=vec(g_pre_ffn), g_post_ffn=vec(g_post_ffn), conv_b=vec(conv_b), b_rgate=vec(b_rgate),
        b_igate=vec(b_igate), lru_lambda=vec(lru_lambda), conv_w=conv_w,
        w_in=w_in.astype(BF16), w_out=w_out.astype(BF16),
        w_rgate=_block_diag(w_rgate).astype(BF16), w_igate=_block_diag(w_igate).astype(BF16),
        w_ffn_gate=w_ffn_gate.astype(BF16), w_ffn_up=w_ffn_up.astype(BF16), w_ffn_down=w_ffn_down.astype(BF16),
    )

    slopes = 2.0 ** (-(8.0 / N_HEADS) * jnp.arange(1, N_HEADS + 1, dtype=F32))
    prompt_bias = _attn_bias(slopes)
    sample_bias, sample_cnt = _sample_tables(slopes)
    cache_kt = cache_k.transpose(0, 1, 3, 4, 2)
    cache_vt = cache_v.transpose(0, 1, 3, 4, 2)

    tm_p = 512
    n_tiles = (batch * seq) // tm_p
    n_att_steps = batch * N_PAIRS * (seq // ATT_CHUNK)
    n_scan_steps = batch * (seq // SCAN_CHUNK)
    hosts = [(0, 1), (n_scan_steps, 2), (n_scan_steps + 2 * n_att_steps, 1)]
    assert n_scan_steps + 2 * n_att_steps + n_tiles == dec_batch
    xp = x_prompt.reshape(batch * seq, d_model)
    xs = x_sample.reshape(dec_batch, d_model)
    kp, vp, cp, hp, ks, vs, cs, hs = [], [], [], [], [], [], [], []
    for l in range(depth):
        (s32,) = _proj(xs, lw["g_pre_mix"], lw["w_in"], l, dec_batch, False)
        sample = (s32[:, :3 * w].T, cache_kt, cache_vt, sample_bias, sample_cnt, l)
        slabs, rg = _proj(xp, lw["g_pre_mix"], lw["w_in"], l, tm_p, True)
        att, att_s1 = _attention(slabs, prompt_bias, batch, seq, (hosts[1],) + sample)
        rec, h_last, att_s0 = _scan(rg, 0, lw, l, batch, seq, (hosts[0],) + sample)
        (xp,) = _mix(att, rec, xp, lw, l, tm_p)
        xp, att_s2 = _ffn(xp, lw, l, tm_p, (hosts[2],) + sample)
        kp.append(_heads_from_slabs(slabs, N_PAIRS, batch, seq, n_keep))
        vp.append(_heads_from_slabs(slabs, 2 * N_PAIRS, batch, seq, n_keep))
        cp.append(rg.reshape(batch, seq, 2 * w)[:, seq - (CONV_WIDTH - 1):, :w])
        hp.append(h_last)
        att_s = (att_s0 + att_s1 + att_s2).T
        rec_s, h_s, conv_s = _step(s32, 3, state_conv, state_h, lw, l)
        (xs,) = _mix(att_s, rec_s, xs, lw, l, dec_batch)
        (xs,) = _ffn(xs, lw, l, dec_batch)
        ks.append(s32[:, w:2 * w].reshape(dec_batch, 1, N_HEADS, HEAD_DIM))
        vs.append(s32[:, 2 * w:3 * w].reshape(dec_batch, 1, N_HEADS, HEAD_DIM))
        cs.append(conv_s.reshape(dec_batch, CONV_WIDTH - 1, w))
        hs.append(h_s)
    return (xp.reshape(batch, seq, d_model), xs.reshape(dec_batch, 1, d_model),
            jnp.stack(kp), jnp.stack(vp), jnp.stack(cp), jnp.stack(hp),
            jnp.stack(ks), jnp.stack(vs), jnp.stack(cs), jnp.stack(hs))
```

```python
import functools

import jax
import jax.numpy as jnp
from jax import lax
from jax.experimental import pallas as pl
from jax.experimental.pallas import tpu as pltpu

F32 = jnp.float32
BF16 = jnp.bfloat16

N_HEADS = 8
HEAD_DIM = 64
ATT_WIDTH = N_HEADS * HEAD_DIM
N_PAIRS = N_HEADS // 2
CONV_WIDTH = 4
LRU_C = 8.0
DILATED = ((128, 1), (512, 4), (2048, 16))
MAX_WINDOW = 2048
BLK = 128
EPS = 1e-6
NEG = -1e30
Q_SCALE = HEAD_DIM ** -0.5
LOG2E = 1.4426950408889634

VMEM_LIMIT_BYTES = 56 * 1024 * 1024
FFN_CHUNKS = ((0, 1536), (1536, 2816))
SCAN_CHUNK = 512
ATT_CHUNK = BLK * DILATED[-1][1]
ATT_GROUP = 4


def _params(*sem):
    return pltpu.CompilerParams(dimension_semantics=sem, vmem_limit_bytes=VMEM_LIMIT_BYTES)


def _const_spec(shape, index):
    return pl.BlockSpec(shape, index, pipeline_mode=pl.Buffered(1))


def _rms(x, g):
    return x * lax.rsqrt(jnp.mean(x * x, axis=-1, keepdims=True) + EPS) * g


def _dot(a, b):
    return jnp.dot(a, b, preferred_element_type=F32)


def _proj_body(x_ref, g_ref, w_ref, *rest, slabs, sample):
    if sample:
        sample_refs, out_refs = rest[:7], rest[7:-1]
        _sample_rows(sample, pl.program_id(0), *sample_refs, rest[-1])
    else:
        out_refs = rest
    h = _rms(x_ref[...], g_ref[...]).astype(BF16)
    n = w_ref.shape[1]
    n_att = 3 * ATT_WIDTH
    for c in range(n // ATT_WIDTH):
        lo, hi = c * ATT_WIDTH, (c + 1) * ATT_WIDTH
        pc = _dot(h, w_ref[:, lo:hi])
        if not slabs:
            out_refs[0][:, lo:hi] = pc
        elif lo >= n_att:
            out_refs[1][:, lo - n_att:hi - n_att] = pc
        else:
            for p in range(N_PAIRS):
                part = pc[:, p * 128:(p + 1) * 128]
                out_refs[0][c * N_PAIRS + p] = part * (Q_SCALE * LOG2E) if c == 0 else part


def _proj(x, g, w, layer, tm, slabs, sample=None):
    m, d = x.shape
    n = w.shape[-1]
    if slabs:
        n_rest = n - 3 * ATT_WIDTH
        out_specs = [pl.BlockSpec((3 * N_PAIRS, tm, 128), lambda i: (0, i, 0)),
                     pl.BlockSpec((tm, n_rest), lambda i: (i, 0))]
        out_shape = [jax.ShapeDtypeStruct((3 * N_PAIRS, m, 128), F32),
                     jax.ShapeDtypeStruct((m, n_rest), F32)]
    else:
        out_specs = [pl.BlockSpec((tm, n), lambda i: (i, 0))]
        out_shape = [jax.ShapeDtypeStruct((m, n), F32)]
    in_specs = [
        pl.BlockSpec((tm, d), lambda i: (i, 0)),
        _const_spec((None, 1, d), lambda i: (layer, 0, 0)),
        _const_spec((None, d, n), lambda i: (layer, 0, 0)),
    ]
    args = [x, g, w]
    if sample:
        sa_specs, sa_args, sa_out_spec, sa_out_shape = _sample_operands(sample, (m // tm,))
        in_specs += sa_specs
        args += sa_args
        out_specs.append(sa_out_spec)
        out_shape.append(sa_out_shape)
    return pl.pallas_call(
        functools.partial(_proj_body, slabs=slabs, sample=sample[0] if sample else None),
        grid=(m // tm,),
        in_specs=in_specs,
        out_specs=out_specs,
        out_shape=out_shape,
        compiler_params=_params("arbitrary" if sample else "parallel"),
        name="proj_in",
    )(*args)


def _attn_body(q_ref, k_ref, v_ref, kp_ref, vp_ref, bias_ref, *rest, sample):
    if sample:
        grid = tuple(pl.num_programs(a) for a in range(3))
        step = _linear_step(grid)(*(pl.program_id(a) for a in range(3)))
        _sample_rows(sample, step, *rest[:7], rest[-5])
        rest = rest[7:-5] + rest[-4:]
    o_ref, t_a, t_b, m_a, m_b = rest
    t_s = (t_a, t_b)
    m_s = (m_a, m_b)
    c = pl.program_id(2)
    C = q_ref.shape[0]
    nt = (((1,), (1,)), ((), ()))
    first_head = lax.broadcasted_iota(jnp.int32, (1, 128), 1) < HEAD_DIM
    heads = (first_head, jnp.logical_not(first_head))

    def rows(start, d):
        return pl.ds(start, BLK) if d == 1 else pl.ds(start, BLK, stride=d)

    def prev_own(cur_ref, prev_ref, st, d, head_of_chunk):
        own = cur_ref[rows(st, d), :]
        prev = prev_ref[rows(C - BLK * d + st, d), :] if head_of_chunk else cur_ref[rows(st - BLK * d, d), :]
        return jnp.concatenate([prev.astype(BF16), own.astype(BF16)], axis=0)

    def run_blocks(bi, d, blocks, init):
        g = len(blocks)
        qs = [q_ref[rows(st, d), :] for st, _ in blocks]
        ks = [prev_own(k_ref, kp_ref, st, d, hd) for st, hd in blocks]
        vs = [prev_own(v_ref, vp_ref, st, d, hd) for st, hd in blocks]
        variant = [(c == 0).astype(jnp.int32) if hd else 0 for _, hd in blocks]
        ss = []
        for i in range(g):
            for h in range(2):
                qm = jnp.where(heads[h], qs[i], 0.0).astype(BF16)
                ss.append(lax.dot_general(qm, ks[i], nt, preferred_element_type=F32) + bias_ref[bi, h, variant[i]])
        ms = [jnp.max(a, axis=-1, keepdims=True) for a in ss]
        ps = [jnp.exp2(a - m).astype(BF16) for a, m in zip(ss, ms)]
        ts = [_dot(ps[2 * i + h], jnp.where(heads[h], vs[i], jnp.ones_like(vs[i])))
              for i in range(g) for h in range(2)]
        for i, (st, _) in enumerate(blocks):
            r = rows(st, d)
            for h in range(2):
                m_t = jnp.broadcast_to(ms[2 * i + h], (BLK, 128))
                t_t = ts[2 * i + h]
                if init:
                    m_s[h][r, :] = m_t
                    t_s[h][r, :] = t_t
                else:
                    m_old = m_s[h][r, :]
                    m_new = jnp.maximum(m_old, m_t)
                    m_s[h][r, :] = m_new
                    t_s[h][r, :] = t_s[h][r, :] * jnp.exp2(m_old - m_new) + t_t * jnp.exp2(m_t - m_new)

    n_groups = C // (BLK * ATT_GROUP)
    order = sorted(range(len(DILATED)), key=lambda b: -DILATED[b][1])
    for pos, bi in enumerate(order):
        d = DILATED[bi][1]
        per_res = C // (BLK * d)
        init = pos == 0

        def group(gi, bi=bi, d=d, per_res=per_res, init=init):
            blocks = []
            for u in range(ATT_GROUP):
                j = gi * ATT_GROUP + u
                if per_res <= ATT_GROUP:
                    n = u % per_res
                    blocks.append((n * (BLK * d) + j // per_res, n == 0))
                else:
                    blocks.append((j * (BLK * d), isinstance(j, int) and j == 0))
            run_blocks(bi, d, blocks, init)

        if per_res <= ATT_GROUP:
            assert ATT_GROUP % per_res == 0
            lax.fori_loop(0, n_groups, lambda gi, carry: (group(gi), carry)[1], 0)
        else:
            assert per_res == C // BLK
            group(0)
            lax.fori_loop(1, n_groups, lambda gi, carry: (group(gi), carry)[1], 0)

    ta = t_a[...]
    tb = t_b[...]
    o_ref[...] = jnp.where(first_head, ta / pltpu.roll(ta, HEAD_DIM, 1),
                           tb / pltpu.roll(tb, HEAD_DIM, 1)).astype(o_ref.dtype)


def _attention(slabs, bias, batch, seq, sample=None):
    C = ATT_CHUNK
    nch = seq // C
    grid = (batch, N_PAIRS, nch)

    def slab(off, back=0):
        return pl.BlockSpec((None, C, 128), lambda b, p, c: (off + p, b * nch + jnp.maximum(c - back, 0), 0))

    state = pltpu.VMEM((C, 128), F32)
    in_specs = [slab(0), slab(N_PAIRS), slab(2 * N_PAIRS), slab(N_PAIRS, 1), slab(2 * N_PAIRS, 1),
                pl.BlockSpec((len(DILATED), None, 2, 2, BLK, 2 * BLK), lambda b, p, c: (0, p, 0, 0, 0, 0))]
    args = [slabs, slabs, slabs, slabs, slabs, bias]
    out_specs = [pl.BlockSpec((None, C, 128), lambda b, p, c: (p, b * nch + c, 0))]
    out_shape = [jax.ShapeDtypeStruct((N_PAIRS, batch * seq, 128), BF16)]
    if sample:
        sa_specs, sa_args, sa_out_spec, sa_out_shape = _sample_operands(sample, grid)
        in_specs += sa_specs
        args += sa_args
        out_specs.append(sa_out_spec)
        out_shape.append(sa_out_shape)
    return pl.pallas_call(
        functools.partial(_attn_body, sample=sample[0] if sample else None),
        grid=grid,
        in_specs=in_specs,
        out_specs=out_specs,
        out_shape=out_shape,
        scratch_shapes=[state, state, state, state],
        compiler_params=_params(*(("arbitrary",) * 3 if sample else ("parallel", "parallel", "arbitrary"))),
        name="attn",
    )(*args)


def _attn_bias(slopes):
    i = jnp.arange(BLK)[:, None]
    j = jnp.arange(2 * BLK)[None, :]
    steps = i + BLK - j
    ok = (steps >= 0) & (steps <= BLK)
    out = []
    for _, dil in DILATED:
        b = (-LOG2E * slopes)[:, None, None] * (dil * steps).astype(F32)[None]
        out.append(jnp.stack([jnp.where(m[None], b, NEG) for m in (ok, ok & (j >= BLK))], axis=1))
    return jnp.stack(out).reshape(len(DILATED), N_PAIRS, 2, 2, BLK, 2 * BLK)


def _softplus(x):
    return jnp.maximum(x, 0.0) + jnp.log1p(jnp.exp(-jnp.abs(x)))


def _gelu(x):
    return 0.5 * x * (1.0 + jnp.tanh(0.7978845608028654 * (x + 0.044715 * (x * x * x))))


def _sigmoid(x):
    return 0.5 * jnp.tanh(0.5 * x) + 0.5


def _lru_terms(xc, wr_ref, br_ref, wi_ref, bi_ref, lam_ref):
    xb = xc.astype(BF16)
    r = _sigmoid(_dot(xb, wr_ref[...]) + br_ref[...])
    i = _sigmoid(_dot(xb, wi_ref[...]) + bi_ref[...])
    log_a = (-LRU_C * _softplus(-lam_ref[...])) * r
    a = jnp.exp(log_a)
    b = jnp.sqrt(-jnp.tanh(log_a) * (a * a + 1.0)) * (i * xc)
    return a, b


def _scan_body(xr_ref, yg_ref, cw_ref, cb_ref, wr_ref, br_ref, wi_ref, bi_ref, lam_ref, *rest, sample):
    if sample:
        step = pl.program_id(0) * pl.num_programs(1) + pl.program_id(1)
        _sample_rows(sample, step, *rest[:7], rest[9])
        rest = rest[7:9] + rest[10:]
    rec_ref, hl_ref, xbuf, a_s, b_s, h_s = rest
    tc = xr_ref.shape[0]

    @pl.when(pl.program_id(1) == 0)
    def _():
        xbuf[0:8, :] = jnp.zeros((8, xbuf.shape[1]), F32)
        h_s[...] = jnp.zeros_like(h_s)

    xbuf[8:, :] = xr_ref[...]
    xc = cb_ref[...]
    for j in range(CONV_WIDTH):
        xc = xc + xbuf[pl.ds(8 - (CONV_WIDTH - 1) + j, tc), :] * cw_ref[j:j + 1, :]
    xbuf[0:8, :] = xbuf[tc:tc + 8, :]

    a, b = _lru_terms(xc, wr_ref, br_ref, wi_ref, bi_ref, lam_ref)
    a_s[...] = a
    b_s[...] = b
    row = lax.broadcasted_iota(jnp.int32, (8, a.shape[1]), 0)

    def group(g, h):
        r0 = pl.multiple_of(g * 8, 8)
        ca = a_s[pl.ds(r0, 8), :]
        cb = b_s[pl.ds(r0, 8), :]
        for s in (1, 2, 4):
            keep = row >= s
            cb = jnp.where(keep, ca * pltpu.roll(cb, s, 0) + cb, cb)
            ca = jnp.where(keep, ca * pltpu.roll(ca, s, 0), ca)
        h8 = ca * h + cb
        b_s[pl.ds(r0, 8), :] = h8
        return h8[7:8, :]

    h = lax.fori_loop(0, tc // 8, group, h_s[...], unroll=8)
    h_s[...] = h
    hl_ref[...] = h
    rec_ref[...] = _gelu(yg_ref[...]) * b_s[...]


def _scan(p32, xr_col, lw, layer, batch, seq, sample=None):
    tc = SCAN_CHUNK
    nc = seq // tc
    w = ATT_WIDTH
    view = p32.reshape(batch, seq, p32.shape[-1])

    def vec(name):
        return _const_spec((None, 1, w), lambda b, c: (layer, 0, 0)), lw[name]

    specs, args = [], []
    for spec, arr in (
        (pl.BlockSpec((None, tc, w), lambda b, c: (b, c, xr_col)), view),
        (pl.BlockSpec((None, tc, w), lambda b, c: (b, c, xr_col + 1)), view),
        (_const_spec((None, CONV_WIDTH, w), lambda b, c: (layer, 0, 0)), lw["conv_w"]),
        vec("conv_b"),
        (_const_spec((None, w, w), lambda b, c: (layer, 0, 0)), lw["w_rgate"]),
        vec("b_rgate"),
        (_const_spec((None, w, w), lambda b, c: (layer, 0, 0)), lw["w_igate"]),
        vec("b_igate"),
        vec("lru_lambda"),
    ):
        specs.append(spec)
        args.append(arr)
    out_specs = [
        pl.BlockSpec((None, tc, w), lambda b, c: (b, c, 0)),
        pl.BlockSpec((None, 1, w), lambda b, c: (b, 0, 0)),
    ]
    out_shape = [
        jax.ShapeDtypeStruct((batch, seq, w), F32),
        jax.ShapeDtypeStruct((batch, 1, w), F32),
    ]
    if sample:
        sa_specs, sa_args, sa_out_spec, sa_out_shape = _sample_operands(sample, (batch, nc))
        specs += sa_specs
        args += sa_args
        out_specs.append(sa_out_spec)
        out_shape.append(sa_out_shape)
    rec, h_last, *sa_out = pl.pallas_call(
        functools.partial(_scan_body, sample=sample[0] if sample else None),
        grid=(batch, nc),
        in_specs=specs,
        out_specs=out_specs,
        out_shape=out_shape,
        scratch_shapes=[
            pltpu.VMEM((tc + 8, w), F32),
            pltpu.VMEM((tc, w), F32),
            pltpu.VMEM((tc, w), F32),
            pltpu.VMEM((1, w), F32),
        ],
        compiler_params=_params(*(("arbitrary", "arbitrary") if sample else ("parallel", "arbitrary"))),
        name="conv_rglru_scan",
    )(*args)
    return (rec.reshape(batch * seq, w), h_last.reshape(batch, w), *sa_out)


def _step_body(xr_ref, yg_ref, sc_ref, h0_ref, cw_ref, cb_ref, wr_ref, br_ref, wi_ref, bi_ref, lam_ref,
               rec_ref, h_ref, conv_ref):
    w = xr_ref.shape[1]
    xr = xr_ref[...]
    xc = cb_ref[...] + xr * cw_ref[CONV_WIDTH - 1:CONV_WIDTH, :]
    for j in range(CONV_WIDTH - 1):
        xc = xc + sc_ref[:, j * w:(j + 1) * w] * cw_ref[j:j + 1, :]
    a, b = _lru_terms(xc, wr_ref, br_ref, wi_ref, bi_ref, lam_ref)
    h = a * h0_ref[...] + b
    h_ref[...] = h
    rec_ref[...] = _gelu(yg_ref[...]) * h
    conv_ref[:, 0:(CONV_WIDTH - 2) * w] = sc_ref[:, w:(CONV_WIDTH - 1) * w]
    conv_ref[:, (CONV_WIDTH - 2) * w:] = xr


def _step(p32, xr_col, state_conv, state_h, lw, layer):
    m = p32.shape[0]
    w = ATT_WIDTH
    sc = state_conv.reshape(state_conv.shape[0], m, (CONV_WIDTH - 1) * w)

    def vec(name):
        return _const_spec((None, 1, w), lambda i: (layer, 0, 0)), lw[name]

    specs, args = [], []
    for spec, arr in (
        (pl.BlockSpec((m, w), lambda i: (0, xr_col)), p32),
        (pl.BlockSpec((m, w), lambda i: (0, xr_col + 1)), p32),
        (pl.BlockSpec((None, m, (CONV_WIDTH - 1) * w), lambda i: (layer, 0, 0)), sc),
        (pl.BlockSpec((None, m, w), lambda i: (layer, 0, 0)), state_h),
        (_const_spec((None, CONV_WIDTH, w), lambda i: (layer, 0, 0)), lw["conv_w"]),
        vec("conv_b"),
        (_const_spec((None, w, w), lambda i: (layer, 0, 0)), lw["w_rgate"]),
        vec("b_rgate"),
        (_const_spec((None, w, w), lambda i: (layer, 0, 0)), lw["w_igate"]),
        vec("b_igate"),
        vec("lru_lambda"),
    ):
        specs.append(spec)
        args.append(arr)
    return pl.pallas_call(
        _step_body,
        grid=(1,),
        in_specs=specs,
        out_specs=[
            pl.BlockSpec((m, w), lambda i: (0, 0)),
            pl.BlockSpec((m, w), lambda i: (0, 0)),
            pl.BlockSpec((m, (CONV_WIDTH - 1) * w), lambda i: (0, 0)),
        ],
        out_shape=[
            jax.ShapeDtypeStruct((m, w), F32),
            jax.ShapeDtypeStruct((m, w), F32),
            jax.ShapeDtypeStruct((m, (CONV_WIDTH - 1) * w), F32),
        ],
        compiler_params=_params("arbitrary"),
        name="conv_rglru_step",
    )(*args)


def _sample_rows(cfg, step, qt_ref, knt_ref, vnt_ref, k_ref, v_ref, bias_ref, cnt_ref, o_ref):
    first_row, n_rows = cfg
    _, nh, hd, _ = k_ref.shape
    lane = lax.broadcasted_iota(jnp.int32, qt_ref.shape, 1)

    @pl.when(step == 0)
    def _():
        o_ref[...] = jnp.zeros_like(o_ref)

    for j in range(n_rows):
        mine = lane == first_row + step * n_rows + j

        def column(ref):
            return jnp.sum(jnp.where(mine, ref[...], 0.0), axis=-1, keepdims=True).reshape(nh, hd, 1)

        q = column(qt_ref) * Q_SCALE
        k_new = column(knt_ref)
        v_new = column(vnt_ref)
        s = jnp.sum(k_ref[j] * q, axis=1, keepdims=True) + bias_ref[...]
        s_new = jnp.sum(k_new * q, axis=1, keepdims=True)
        m = jnp.maximum(jnp.max(s, axis=-1, keepdims=True), s_new)
        p = jnp.exp(s - m) * cnt_ref[...]
        p_new = jnp.exp(s_new - m) * float(len(DILATED))
        l = jnp.sum(p, axis=-1, keepdims=True) + p_new
        acc = jnp.sum(p * v_ref[j], axis=-1, keepdims=True) + p_new * v_new
        o_ref[...] = jnp.where(mine, (acc / l).reshape(nh * hd, 1), o_ref[...])


def _linear_step(grid):
    def lin(*idx):
        out = 0
        for i, n in zip(idx, grid):
            out = out * n + i
        return out
    return lin


def _sample_operands(sample_args, grid):
    (first_row, n_rows), qkv_t, cache_kt, cache_vt, bias, cnt, layer = sample_args
    _, batch, nh, hd, w_buf = cache_kt.shape
    w = nh * hd
    win = bias.shape[-1]
    n_steps = 1
    for n in grid:
        n_steps *= n
    assert first_row % n_rows == 0 and first_row + n_rows * n_steps <= batch
    lin = _linear_step(grid)
    col = lambda c: _const_spec((w, batch), lambda *i: (c, 0))
    cache = pl.BlockSpec((None, n_rows, nh, hd, win),
                         lambda *i: (layer, first_row // n_rows + lin(*i), 0, 0, w_buf // win - 1))
    specs = [col(0), col(1), col(2), cache, cache,
             _const_spec((nh, 1, win), lambda *i: (0, 0, 0)), _const_spec((1, 1, win), lambda *i: (0, 0, 0))]
    args = [qkv_t, qkv_t, qkv_t, cache_kt, cache_vt, bias, cnt]
    return specs, args, pl.BlockSpec((w, batch), lambda *i: (0, 0)), jax.ShapeDtypeStruct((w, batch), F32)


def _sample_tables(slopes):
    dist = MAX_WINDOW - jnp.arange(MAX_WINDOW, dtype=jnp.int32)
    cnt = sum(((dist % dil == 0) & (dist <= win)).astype(F32) for win, dil in DILATED)
    bias = jnp.where(cnt > 0, -slopes[:, None] * dist.astype(F32)[None, :], NEG)
    return bias[:, None, :], cnt[None, None, :]


def _mix_body(att_ref, rec_ref, x_ref, ga_ref, gl_ref, w_ref, gp_ref, *rest, sample):
    if sample:
        _sample_rows(sample, pl.program_id(0), *rest[:7], rest[-1])
    out_ref = rest[-2] if sample else rest[-1]
    if len(att_ref.shape) == 3:
        att = jnp.concatenate([att_ref[p] for p in range(att_ref.shape[0])], axis=-1).astype(F32)
    else:
        att = att_ref[...]
    half = att.shape[1]
    att_n = _rms(att, ga_ref[...]).astype(BF16)
    rec_n = _rms(rec_ref[...], gl_ref[...]).astype(BF16)
    mixed = _dot(att_n, w_ref[0:half, :]) + _dot(rec_n, w_ref[half:, :])
    out_ref[...] = x_ref[...] + _rms(mixed, gp_ref[...])


def _mix(att, rec, x, lw, layer, tm, sample=None):
    m, d = x.shape
    w = ATT_WIDTH
    row = lambda width: pl.BlockSpec((tm, width), lambda i: (i, 0))
    att_spec = row(w) if att.ndim == 2 else pl.BlockSpec((att.shape[0], tm, 128), lambda i: (0, i, 0))
    in_specs = [
        att_spec, row(w), row(d),
        _const_spec((None, 1, w), lambda i: (layer, 0, 0)),
        _const_spec((None, 1, w), lambda i: (layer, 0, 0)),
        _const_spec((None, 2 * w, d), lambda i: (layer, 0, 0)),
        _const_spec((None, 1, d), lambda i: (layer, 0, 0)),
    ]
    args = [att, rec, x, lw["g_att_out"], lw["g_lru_out"], lw["w_out"], lw["g_post_mix"]]
    out_specs = [row(d)]
    out_shape = [jax.ShapeDtypeStruct((m, d), F32)]
    if sample:
        sa_specs, sa_args, sa_out_spec, sa_out_shape = _sample_operands(sample, (m // tm,))
        in_specs += sa_specs
        args += sa_args
        out_specs.append(sa_out_spec)
        out_shape.append(sa_out_shape)
    return pl.pallas_call(
        functools.partial(_mix_body, sample=sample[0] if sample else None),
        grid=(m // tm,),
        in_specs=in_specs,
        out_specs=out_specs,
        out_shape=out_shape,
        compiler_params=_params("arbitrary" if sample else "parallel"),
        name="mix_out",
    )(*args)


def _ffn_body(x_ref, g1_ref, wg_ref, wu_ref, wd_ref, g2_ref, *rest, sample):
    if sample:
        _sample_rows(sample, pl.program_id(0), *rest[:7], rest[-1])
    out_ref = rest[-2] if sample else rest[-1]
    x = x_ref[...]
    h = _rms(x, g1_ref[...]).astype(BF16)
    acc = None
    for f0, f1 in FFN_CHUNKS:
        gate = _dot(h, wg_ref[:, f0:f1])
        up = _dot(h, wu_ref[:, f0:f1])
        act = (gate * jax.nn.sigmoid(gate) * up).astype(BF16)
        part = _dot(act, wd_ref[f0:f1, :])
        acc = part if acc is None else acc + part
    out_ref[...] = x + _rms(acc, g2_ref[...])


def _ffn(x, lw, layer, tm, sample=None):
    m, d = x.shape
    f = lw["w_ffn_gate"].shape[-1]
    assert FFN_CHUNKS[-1][1] == f
    in_specs = [
        pl.BlockSpec((tm, d), lambda i: (i, 0)),
        _const_spec((None, 1, d), lambda i: (layer, 0, 0)),
        _const_spec((None, d, f), lambda i: (layer, 0, 0)),
        _const_spec((None, d, f), lambda i: (layer, 0, 0)),
        _const_spec((None, f, d), lambda i: (layer, 0, 0)),
        _const_spec((None, 1, d), lambda i: (layer, 0, 0)),
    ]
    args = [x, lw["g_pre_ffn"], lw["w_ffn_gate"], lw["w_ffn_up"], lw["w_ffn_down"], lw["g_post_ffn"]]
    out_specs = [pl.BlockSpec((tm, d), lambda i: (i, 0))]
    out_shape = [jax.ShapeDtypeStruct((m, d), F32)]
    if sample:
        sa_specs, sa_args, sa_out_spec, sa_out_shape = _sample_operands(sample, (m // tm,))
        in_specs += sa_specs
        args += sa_args
        out_specs.append(sa_out_spec)
        out_shape.append(sa_out_shape)
    return pl.pallas_call(
        functools.partial(_ffn_body, sample=sample[0] if sample else None),
        grid=(m // tm,),
        in_specs=in_specs,
        out_specs=out_specs,
        out_shape=out_shape,
        compiler_params=_params("arbitrary" if sample else "parallel"),
        name="ffn",
    )(*args)


def _block_diag(w):
    depth, nblk, c, _ = w.shape
    eye = jnp.eye(nblk, dtype=w.dtype)
    return jnp.einsum("lncd,nm->lncmd", w, eye).reshape(depth, nblk * c, nblk * c)


def _heads_from_slabs(slabs, first, batch, seq, n_keep):
    t = slabs.reshape(slabs.shape[0], batch, seq, 128)[first:first + N_PAIRS, :, seq - n_keep:, :]
    return t.transpose(1, 2, 0, 3).reshape(batch, n_keep, N_HEADS, HEAD_DIM)


def kernel(x_prompt, x_sample, cache_k, cache_v, state_conv, state_h, g_pre_mix, g_post_mix, w_in, w_out, conv_w, conv_b, w_rgate, b_rgate, w_igate, b_igate, lru_lambda, g_att_out, g_lru_out, g_pre_ffn, g_post_ffn, w_ffn_gate, w_ffn_up, w_ffn_down):
    batch, seq, d_model = x_prompt.shape
    dec_batch, dec_seq, _ = x_sample.shape
    depth = w_in.shape[0]
    w = ATT_WIDTH
    assert dec_seq == 1 and seq % ATT_CHUNK == 0 and cache_k.shape[2] % MAX_WINDOW == 0
    n_keep = min(MAX_WINDOW, seq)

    vec = lambda a: a.reshape(depth, 1, a.shape[-1])
    lw = dict(
        g_pre_mix=vec(g_pre_mix), g_post_mix=vec(g_post_mix), g_att_out=vec(g_att_out), g_lru_out=vec(g_lru_out),
        g_pre_ffn=vec(g_pre_ffn), g_post_ffn=vec(g_post_ffn), conv_b=vec(conv_b), b_rgate=vec(b_rgate),
        b_igate=vec(b_igate), lru_lambda=vec(lru_lambda), conv_w=conv_w,
        w_in=w_in.astype(BF16), w_out=w_out.astype(BF16),
        w_rgate=_block_diag(w_rgate).astype(BF16), w_igate=_block_diag(w_igate).astype(BF16),
        w_ffn_gate=w_ffn_gate.astype(BF16), w_ffn_up=w_ffn_up.astype(BF16), w_ffn_down=w_ffn_down.astype(BF16),
    )

    slopes = 2.0 ** (-(8.0 / N_HEADS) * jnp.arange(1, N_HEADS + 1, dtype=F32))
    prompt_bias = _attn_bias(slopes)
    sample_bias, sample_cnt = _sample_tables(slopes)
    cache_kt = cache_k.transpose(0, 1, 3, 4, 2)
    cache_vt = cache_v.transpose(0, 1, 3, 4, 2)

    tm_p = 512
    tm_proj = 1024
    n_tiles = (batch * seq) // tm_p
    n_att_steps = batch * N_PAIRS * (seq // ATT_CHUNK)
    n_scan_steps = batch * (seq // SCAN_CHUNK)
    hosts = [(0, 1), (n_scan_steps, 2), (n_scan_steps + 2 * n_att_steps, 1)]
    assert n_scan_steps + 2 * n_att_steps + n_tiles == dec_batch
    xp = x_prompt.reshape(batch * seq, d_model)
    xs = x_sample.reshape(dec_batch, d_model)
    kp, vp, cp, hp, ks, vs, cs, hs = [], [], [], [], [], [], [], []
    for l in range(depth):
        (s32,) = _proj(xs, lw["g_pre_mix"], lw["w_in"], l, dec_batch, False)
        sample = (s32[:, :3 * w].T, cache_kt, cache_vt, sample_bias, sample_cnt, l)
        slabs, rg = _proj(xp, lw["g_pre_mix"], lw["w_in"], l, tm_proj, True)
        att, att_s1 = _attention(slabs, prompt_bias, batch, seq, (hosts[1],) + sample)
        rec, h_last, att_s0 = _scan(rg, 0, lw, l, batch, seq, (hosts[0],) + sample)
        (xp,) = _mix(att, rec, xp, lw, l, tm_p)
        xp, att_s2 = _ffn(xp, lw, l, tm_p, (hosts[2],) + sample)
        kp.append(_heads_from_slabs(slabs, N_PAIRS, batch, seq, n_keep))
        vp.append(_heads_from_slabs(slabs, 2 * N_PAIRS, batch, seq, n_keep))
        cp.append(rg.reshape(batch, seq, 2 * w)[:, seq - (CONV_WIDTH - 1):, :w])
        hp.append(h_last)
        att_s = (att_s0 + att_s1 + att_s2).T
        rec_s, h_s, conv_s = _step(s32, 3, state_conv, state_h, lw, l)
        (xs,) = _mix(att_s, rec_s, xs, lw, l, dec_batch)
        (xs,) = _ffn(xs, lw, l, dec_batch)
        ks.append(s32[:, w:2 * w].reshape(dec_batch, 1, N_HEADS, HEAD_DIM))
        vs.append(s32[:, 2 * w:3 * w].reshape(dec_batch, 1, N_HEADS, HEAD_DIM))
        cs.append(conv_s.reshape(dec_batch, CONV_WIDTH - 1, w))
        hs.append(h_s)
    return (xp.reshape(batch, seq, d_model), xs.reshape(dec_batch, 1, d_model),
            jnp.stack(kp), jnp.stack(vp), jnp.stack(cp), jnp.stack(hp),
            jnp.stack(ks), jnp.stack(vs), jnp.stack(cs), jnp.stack(hs))
```
